```python
import math
import jax, jax.numpy as jnp
from jax import lax
import numpy as np

D_MODEL = 1024
BATCH = 16
SEQ = 4096
DEPTH = 4

N_MIXERS = 3
N_CONV_LAYERS = (DEPTH + 2) // 3
N_POOL_LAYERS = (DEPTH + 1) // 3
N_NSA_LAYERS = DEPTH // 3

CONV_WIDTH = 3
D_FF = 2816

POOL_WINDOWS = (2, 4, 8, 16)
N_POOL_GROUPS = 4
POOL_GROUP = D_MODEL // N_POOL_GROUPS

N_HEADS = 16
HEAD_DIM = D_MODEL // N_HEADS
N_KV_GROUPS = 4
HEADS_PER_GROUP = N_HEADS // N_KV_GROUPS
KV_WIDTH = N_KV_GROUPS * HEAD_DIM
CMP_BLOCK = 32
CMP_STRIDE = 16
CMP_HIDDEN = 256
SEL_BLOCK = 64
N_SELECT = 16
WINDOW = 512
Q_BLOCK = 128
N_BRANCH = 3
NSA_IN = D_MODEL + 6 * KV_WIDTH + N_BRANCH * N_HEADS

ALPHA = (2.0 * DEPTH) ** 0.25
BETA = (8.0 * DEPTH) ** -0.25
LN_EPS = 1e-5
NEG_INF = -1e30
FORCE_SCORE = 1e4

kernel_name = "hybrid_conv_pool_nsa_deepnorm_adaln"


def layer_norm(x, g, b):
    xf = x.astype(jnp.float32)
    mu = jnp.mean(xf, -1, keepdims=True)
    var = jnp.mean(jnp.square(xf - mu), -1, keepdims=True)
    y = (xf - mu) * lax.rsqrt(var + LN_EPS)
    return (y * g.astype(jnp.float32) + b.astype(jnp.float32)).astype(x.dtype)


def causal_dwconv(x, w):
    k = w.shape[0]
    return lax.conv_general_dilated(
        x, w[:, None, :].astype(x.dtype), window_strides=(1,), padding=((k - 1, 0),),
        dimension_numbers=("NWC", "WIO", "NWC"), feature_group_count=x.shape[-1])


def short_conv_mixer(h, w_in, w_conv, w_out):
    b_gate, c_gate, u = jnp.split(h @ w_in, 3, axis=-1)
    return (b_gate * causal_dwconv(c_gate * u, w_conv)) @ w_out


def causal_window_mean(u, w):
    t = u.shape[1]
    cs = jnp.cumsum(u.astype(jnp.float32), axis=1)
    prev = jnp.pad(cs, ((0, 0), (w, 0), (0, 0)))[:, :t]
    cnt = jnp.minimum(jnp.arange(1, t + 1, dtype=jnp.float32), float(w))
    return ((cs - prev) / cnt[None, :, None]).astype(u.dtype)


def pool_mixer(h, w_in, w_grp, scale, w_out):
    bsz, t, _ = h.shape
    u = (h @ w_in).reshape(bsz, t, N_POOL_GROUPS, POOL_GROUP)
    pooled = jnp.stack([causal_window_mean(u[:, :, g], w) - u[:, :, g]
                        for g, w in enumerate(POOL_WINDOWS)], axis=2)
    z = jnp.einsum("btgc,gce->btge", pooled, w_grp).reshape(bsz, t, D_MODEL)
    return (z * scale) @ w_out


def alibi_slopes():
    hh = jnp.arange(1, N_HEADS + 1, dtype=jnp.float32)
    return jnp.exp2(-8.0 * hh / N_HEADS).reshape(N_KV_GROUPS, HEADS_PER_GROUP)


def nsa_mixer(h, w_in, cmp_pos_k, cmp_pos_v, cmp_k_w1, cmp_k_w2, cmp_v_w1, cmp_v_w2, w_out):
    bsz, t, _ = h.shape
    dt = h.dtype
    f32 = jnp.float32
    G, R, HD = N_KV_GROUPS, HEADS_PER_GROUP, HEAD_DIM
    splits = np.cumsum([D_MODEL] + [KV_WIDTH] * 6).tolist()
    q, kc, vc, ks, vs, kw, vw, gl = jnp.split(h @ w_in, splits, axis=-1)
    q = q.reshape(bsz, t, G, R, HD)
    kc, vc, ks, vs, kw, vw = [a.reshape(bsz, t, G, HD) for a in (kc, vc, ks, vs, kw, vw)]
    gates = jax.nn.sigmoid(gl).reshape(bsz, t, G, R, N_BRANCH)

    n_cmp = (t - CMP_BLOCK) // CMP_STRIDE + 1
    blk_idx = jnp.arange(n_cmp)[:, None] * CMP_STRIDE + jnp.arange(CMP_BLOCK)[None, :]

    def compress(a, pos, w1, w2):
        blocks = a[:, blk_idx] + pos[None, None, :, None, :]
        hid = jax.nn.gelu(jnp.einsum("bnlgd,lde->bnge", blocks, w1))
        return jnp.einsum("bnge,ed->bngd", hid, w2)

    k_cmp = compress(kc, cmp_pos_k, cmp_k_w1, cmp_k_w2)
    v_cmp = compress(vc, cmp_pos_v, cmp_v_w1, cmp_v_w2)
    cmp_start = jnp.arange(n_cmp, dtype=jnp.int32) * CMP_STRIDE
    cmp_end = cmp_start + (CMP_BLOCK - 1)
    cmp_centre = cmp_start.astype(f32) + 0.5 * (CMP_BLOCK - 1)

    n_sel_blocks = t // SEL_BLOCK
    n_sel = min(N_SELECT, n_sel_blocks)
    sel_start = jnp.arange(n_sel_blocks, dtype=jnp.int32) * SEL_BLOCK
    overlap = ((cmp_start[:, None] <= sel_start[None, :] + SEL_BLOCK - 1)
               & (cmp_end[:, None] >= sel_start[None, :])).astype(f32)
    k_sel = ks.reshape(bsz, n_sel_blocks, SEL_BLOCK, G, HD).transpose(0, 3, 1, 2, 4)
    v_sel = vs.reshape(bsz, n_sel_blocks, SEL_BLOCK, G, HD).transpose(0, 3, 1, 2, 4)

    k_win = jnp.pad(kw, ((0, 0), (WINDOW, 0), (0, 0), (0, 0)))
    v_win = jnp.pad(vw, ((0, 0), (WINDOW, 0), (0, 0), (0, 0)))

    slopes = alibi_slopes()
    scale = HEAD_DIM ** -0.5
    n_qb = t // Q_BLOCK
    g_ar = jnp.arange(G)[:, None, None]
    sel_off = jnp.arange(SEL_BLOCK, dtype=jnp.int32)
    win_off = jnp.arange(Q_BLOCK + WINDOW, dtype=jnp.int32) - WINDOW

    def step(args):
        b, q0, qb, gb = args
        tq = q0 + jnp.arange(Q_BLOCK, dtype=jnp.int32)
        tf = tq.astype(f32)

        kc_b = lax.dynamic_index_in_dim(k_cmp, b, 0, keepdims=False)
        vc_b = lax.dynamic_index_in_dim(v_cmp, b, 0, keepdims=False)
        s_c = jnp.einsum("qgrd,ngd->grqn", qb, kc_b).astype(f32) * scale
        dist_c = tf[:, None] - cmp_centre[None, :]
        valid_c = cmp_end[None, :] <= tq[:, None]
        s_c = jnp.where(valid_c, s_c - slopes[:, :, None, None] * dist_c, NEG_INF)
        has_c = (tq >= CMP_BLOCK - 1).astype(f32)[None, None, :, None]
        p_c = jax.nn.softmax(s_c, axis=-1) * has_c
        o_c = jnp.einsum("grqn,ngd->qgrd", p_c.astype(dt), vc_b)

        imp = jnp.einsum("grqn,nj->gqj", p_c, overlap)
        bt = tq // SEL_BLOCK
        jb = jnp.arange(n_sel_blocks, dtype=jnp.int32)[None, :]
        forced = (jb == 0) | (jb == bt[:, None]) | (jb == bt[:, None] - 1)
        causal_b = jb <= bt[:, None]
        sel_score = jnp.where(forced[None], FORCE_SCORE, jnp.where(causal_b[None], imp, -FORCE_SCORE))
        _, idx = lax.top_k(sel_score, n_sel)

        ks_b = lax.dynamic_index_in_dim(k_sel, b, 0, keepdims=False)
        vs_b = lax.dynamic_index_in_dim(v_sel, b, 0, keepdims=False)
        kg = ks_b[g_ar, idx]
        vg = vs_b[g_ar, idx]
        s_s = jnp.einsum("qgrd,gqksd->grqks", qb, kg).astype(f32) * scale
        dist_s = tq[None, :, None, None] - (idx[..., None] * SEL_BLOCK + sel_off)
        s_s = jnp.where((dist_s >= 0)[:, None],
                        s_s - slopes[:, :, None, None, None] * dist_s.astype(f32)[:, None], NEG_INF)
        p_s = jax.nn.softmax(s_s.reshape(G, R, Q_BLOCK, n_sel * SEL_BLOCK), axis=-1)
        p_s = p_s.reshape(G, R, Q_BLOCK, n_sel, SEL_BLOCK)
        o_s = jnp.einsum("grqks,gqksd->qgrd", p_s.astype(dt), vg)

        kw_b = lax.dynamic_slice(k_win, (b, q0, 0, 0), (1, Q_BLOCK + WINDOW, G, HD))[0]
        vw_b = lax.dynamic_slice(v_win, (b, q0, 0, 0), (1, Q_BLOCK + WINDOW, G, HD))[0]
        s_pos = q0 + win_off
        dist_w = tq[:, None] - s_pos[None, :]
        valid_w = (dist_w >= 0) & (dist_w < WINDOW) & (s_pos[None, :] >= 0)
        s_w = jnp.einsum("qgrd,kgd->grqk", qb, kw_b).astype(f32) * scale
        s_w = jnp.where(valid_w, s_w - slopes[:, :, None, None] * dist_w.astype(f32), NEG_INF)
        p_w = jax.nn.softmax(s_w, axis=-1)
        o_w = jnp.einsum("grqk,kgd->qgrd", p_w.astype(dt), vw_b)

        o = gb[..., 0:1] * o_c + gb[..., 1:2] * o_s + gb[..., 2:3] * o_w
        return o.reshape(Q_BLOCK, D_MODEL)

    b_idx = jnp.repeat(jnp.arange(bsz, dtype=jnp.int32), n_qb)
    q0s = jnp.tile(jnp.arange(n_qb, dtype=jnp.int32) * Q_BLOCK, bsz)
    q_x = q.reshape(bsz * n_qb, Q_BLOCK, G, R, HD)
    g_x = gates.reshape(bsz * n_qb, Q_BLOCK, G, R, N_BRANCH)
    o = lax.map(step, (b_idx, q0s, q_x, g_x)).reshape(bsz, t, D_MODEL)
    return o @ w_out


def conv_ffn(h, w_in, w_conv, w_out):
    a, v = jnp.split(h @ w_in, 2, axis=-1)
    return (jax.nn.gelu(causal_dwconv(a, w_conv)) * v) @ w_out


def setup_inputs(seed: int = 0) -> dict:
    key = jax.random.key(seed)
    keys = iter(jax.random.split(key, 32))
    D = D_MODEL

    def nrm(shape, s):
        return jax.random.normal(next(keys), shape, jnp.float32) * s

    return {
        "x": nrm((BATCH, SEQ, D), 1.0),
        "c": nrm((BATCH, D), 1.0),
        "ada_w": nrm((DEPTH, D, 6 * D), 0.1 * D ** -0.5),
        "ada_b": nrm((DEPTH, 6 * D), 0.02),
        "ln1_g": 1.0 + nrm((DEPTH, D), 0.02),
        "ln1_b": nrm((DEPTH, D), 0.02),
        "ln2_g": 1.0 + nrm((DEPTH, D), 0.02),
        "ln2_b": nrm((DEPTH, D), 0.02),
        "ffn_w_in": nrm((DEPTH, D, 2 * D_FF), D ** -0.5),
        "ffn_conv": nrm((DEPTH, CONV_WIDTH, D_FF), CONV_WIDTH ** -0.5),
        "ffn_w_out": nrm((DEPTH, D_FF, D), BETA * D_FF ** -0.5),
        "conv_w_in": nrm((N_CONV_LAYERS, D, 3 * D), D ** -0.5),
        "conv_w": nrm((N_CONV_LAYERS, CONV_WIDTH, D), CONV_WIDTH ** -0.5),
        "conv_w_out": nrm((N_CONV_LAYERS, D, D), BETA * D ** -0.5),
        "pool_w_in": nrm((N_POOL_LAYERS, D, D), D ** -0.5),
        "pool_w_grp": nrm((N_POOL_LAYERS, N_POOL_GROUPS, POOL_GROUP, POOL_GROUP), POOL_GROUP ** -0.5),
        "pool_scale": 1.0 + nrm((N_POOL_LAYERS, D), 0.02),
        "pool_w_out": nrm((N_POOL_LAYERS, D, D), BETA * D ** -0.5),
        "nsa_w_in": nrm((N_NSA_LAYERS, D, NSA_IN), D ** -0.5),
        "nsa_cmp_pos_k": nrm((N_NSA_LAYERS, CMP_BLOCK, HEAD_DIM), 0.1),
        "nsa_cmp_pos_v": nrm((N_NSA_LAYERS, CMP_BLOCK, HEAD_DIM), 0.1),
        "nsa_cmp_k_w1": nrm((N_NSA_LAYERS, CMP_BLOCK, HEAD_DIM, CMP_HIDDEN), (CMP_BLOCK * HEAD_DIM) ** -0.5),
        "nsa_cmp_k_w2": nrm((N_NSA_LAYERS, CMP_HIDDEN, HEAD_DIM), CMP_HIDDEN ** -0.5),
        "nsa_cmp_v_w1": nrm((N_NSA_LAYERS, CMP_BLOCK, HEAD_DIM, CMP_HIDDEN), (CMP_BLOCK * HEAD_DIM) ** -0.5),
        "nsa_cmp_v_w2": nrm((N_NSA_LAYERS, CMP_HIDDEN, HEAD_DIM), CMP_HIDDEN ** -0.5),
        "nsa_w_out": nrm((N_NSA_LAYERS, D, D), BETA * D ** -0.5),
    }


def reference(x, c, ada_w, ada_b, ln1_g, ln1_b, ln2_g, ln2_b, ffn_w_in, ffn_conv, ffn_w_out,
              conv_w_in, conv_w, conv_w_out, pool_w_in, pool_w_grp, pool_scale, pool_w_out,
              nsa_w_in, nsa_cmp_pos_k, nsa_cmp_pos_v, nsa_cmp_k_w1, nsa_cmp_k_w2,
              nsa_cmp_v_w1, nsa_cmp_v_w2, nsa_w_out):
    cond = jax.nn.silu(c)
    for i in range(DEPTH):
        mod = (cond @ ada_w[i] + ada_b[i])[:, None, :]
        sh1, sc1, g1, sh2, sc2, g2 = jnp.split(mod, 6, axis=-1)
        m, j = i % N_MIXERS, i // N_MIXERS
        h = x * (1 + sc1) + sh1
        if m == 0:
            y = short_conv_mixer(h, conv_w_in[j], conv_w[j], conv_w_out[j])
        elif m == 1:
            y = pool_mixer(h, pool_w_in[j], pool_w_grp[j], pool_scale[j], pool_w_out[j])
        else:
            y = nsa_mixer(h, nsa_w_in[j], nsa_cmp_pos_k[j], nsa_cmp_pos_v[j], nsa_cmp_k_w1[j],
                          nsa_cmp_k_w2[j], nsa_cmp_v_w1[j], nsa_cmp_v_w2[j], nsa_w_out[j])
        x = layer_norm(ALPHA * x + (1 + g1) * y, ln1_g[i], ln1_b[i])
        h = x * (1 + sc2) + sh2
        x = layer_norm(ALPHA * x + (1 + g2) * conv_ffn(h, ffn_w_in[i], ffn_conv[i], ffn_w_out[i]),
                       ln2_g[i], ln2_b[i])
    return x
```

```python
import functools
import math

import jax
import jax.numpy as jnp
from jax import lax
from jax.experimental import pallas as pl
from jax.experimental.pallas import tpu as pltpu

F32 = jnp.float32
BF16 = jnp.bfloat16

D_MODEL = 1024
BATCH = 16
SEQ = 4096
DEPTH = 4
N_MIXERS = 3
CONV_WIDTH = 3
D_FF = 2816
POOL_WINDOWS = (2, 4, 8, 16)
N_POOL_GROUPS = 4
POOL_GROUP = D_MODEL // N_POOL_GROUPS
N_HEADS = 16
HEAD_DIM = 64
N_KV_GROUPS = 4
HEADS_PER_GROUP = 4
KV_WIDTH = N_KV_GROUPS * HEAD_DIM
CMP_BLOCK = 32
CMP_STRIDE = 16
CMP_HIDDEN = 256
SEL_BLOCK = 64
N_SELECT = 16
WINDOW = 512
N_BRANCH = 3
ALPHA = (2.0 * DEPTH) ** 0.25
LN_EPS = 1e-5
NEG_INF = -1e30
FORCE_SCORE = 1e4

N_CMP = (SEQ - CMP_BLOCK) // CMP_STRIDE + 1
N_CMP_PAD = 256
N_SEL_BLOCKS = SEQ // SEL_BLOCK
N_GATES = N_BRANCH * N_HEADS
LANES = 128
SUBLANES = 8

TM = 512
FC = 256
CONV_HALO = 8
POOL_HALO = 16
TQ = 128
RQ = HEADS_PER_GROUP * TQ
TKS = 256
TKW = 128
VMEM_LIMIT = 56 * 1024 * 1024


def _cparams(n_axes):
    return pltpu.CompilerParams(dimension_semantics=("arbitrary",) * n_axes,
                                vmem_limit_bytes=VMEM_LIMIT)


def _log2(n):
    assert n & (n - 1) == 0
    return n.bit_length() - 1


def _gelu(x):
    c = math.sqrt(2.0 / math.pi)
    return x * (0.5 * (1.0 + jnp.tanh(c * (x + 0.044715 * (x * x * x)))))


def _deepnorm_ln(x, gate, y, ln_g, ln_b):
    r = ALPHA * x + (1.0 + gate) * y
    mu = jnp.mean(r, axis=-1, keepdims=True)
    d = r - mu
    var = jnp.mean(d * d, axis=-1, keepdims=True)
    return d * lax.rsqrt(var + LN_EPS) * ln_g + ln_b


def _split3(a):
    a1 = a.astype(BF16)
    r1 = a - a1.astype(F32)
    a2 = r1.astype(BF16)
    a3 = (r1 - a2.astype(F32)).astype(BF16)
    return a1, a2, a3


def _dot_sel(a, onehot):
    a1, a2, a3 = _split3(a)
    out = jnp.dot(a1, onehot, preferred_element_type=F32)
    out += jnp.dot(a2, onehot, preferred_element_type=F32)
    out += jnp.dot(a3, onehot, preferred_element_type=F32)
    return out


def _ada_kernel(c_ref, w_ref, b_ref, o_ref):
    c = c_ref[...]
    cond = c * jax.nn.sigmoid(c)
    c1, c2, c3 = _split3(cond)
    w1, w2, w3 = _split3(w_ref[0])
    acc = jnp.dot(c1, w1, preferred_element_type=F32)
    acc += jnp.dot(c1, w2, preferred_element_type=F32)
    acc += jnp.dot(c2, w1, preferred_element_type=F32)
    acc += jnp.dot(c1, w3, preferred_element_type=F32)
    acc += jnp.dot(c2, w2, preferred_element_type=F32)
    acc += jnp.dot(c3, w1, preferred_element_type=F32)
    o_ref[0] = acc + b_ref[0]


def _modulation(c, ada_w, ada_b):
    nc = 1536
    n_col = 6 * D_MODEL // nc
    return pl.pallas_call(
        _ada_kernel,
        grid=(DEPTH, n_col),
        in_specs=[
            pl.BlockSpec((BATCH, D_MODEL), lambda i, j: (0, 0)),
            pl.BlockSpec((1, D_MODEL, nc), lambda i, j: (i, 0, j)),
            pl.BlockSpec((1, 1, nc), lambda i, j: (i, 0, j)),
        ],
        out_specs=pl.BlockSpec((1, BATCH, nc), lambda i, j: (i, 0, j)),
        out_shape=jax.ShapeDtypeStruct((DEPTH, BATCH, 6 * D_MODEL), F32),
        compiler_params=_cparams(2),
        name="modulation",
    )(c, ada_w, ada_b.reshape(DEPTH, 1, 6 * D_MODEL))


def _gated_kernel(mode, n_chunks, mrow, x_ref, mod_ref, win_ref, wconv_ref, wout_ref, lng_ref, lnb_ref,
                  o_ref, h_ref, carry_ref, cbuf_ref, acc_ref):
    t = pl.program_id(1)

    @pl.when(t == 0)
    def _():
        carry_ref[...] = jnp.zeros_like(carry_ref)

    xt = x_ref[0]
    m = mod_ref[0]
    sh, sc, gate = m[mrow:mrow + 1], m[mrow + 1:mrow + 2], m[mrow + 2:mrow + 3]
    h_ref[...] = (xt * (1.0 + sc) + sh).astype(BF16)
    acc_ref[...] = jnp.zeros_like(acc_ref)

    def body(j, carry):
        p = jnp.dot(h_ref[...], win_ref[j], preferred_element_type=F32)
        if mode == "conv":
            gate_in = p[:, :FC]
            cin = p[:, FC:2 * FC] * p[:, 2 * FC:]
        else:
            cin = p[:, :FC]
            gate_in = p[:, FC:]
        cbuf_ref[0:CONV_HALO, :] = carry_ref[j]
        cbuf_ref[CONV_HALO:, :] = cin
        carry_ref[j] = cin[TM - CONV_HALO:, :]
        w = wconv_ref[j]
        y = (w[0:1] * cbuf_ref[CONV_HALO - 2:CONV_HALO - 2 + TM, :]
             + w[1:2] * cbuf_ref[CONV_HALO - 1:CONV_HALO - 1 + TM, :]
             + w[2:3] * cin)
        if mode == "conv":
            z = gate_in * y
        else:
            z = _gelu(y) * gate_in
        acc_ref[...] += jnp.dot(z.astype(BF16), wout_ref[j], preferred_element_type=F32)
        return carry

    lax.fori_loop(0, n_chunks, body, 0)
    o_ref[0] = _deepnorm_ln(xt, gate, acc_ref[...], lng_ref[...], lnb_ref[...])


def _gated_layer(mode, x, mod, mrow, w_in, w_conv, w_out, ln_g, ln_b):
    feat = w_out.shape[0]
    n_parts = w_in.shape[1] // feat
    n_chunks = feat // FC
    win_r = (w_in.astype(BF16).reshape(D_MODEL, n_parts, n_chunks, FC)
             .transpose(2, 0, 1, 3).reshape(n_chunks, D_MODEL, n_parts * FC))
    wout_r = w_out.astype(BF16).reshape(n_chunks, FC, D_MODEL)
    wconv_r = w_conv.reshape(CONV_WIDTH, n_chunks, FC).transpose(1, 0, 2)
    const3 = lambda b, t: (0, 0, 0)
    return pl.pallas_call(
        functools.partial(_gated_kernel, mode, n_chunks, mrow),
        grid=(BATCH, SEQ // TM),
        in_specs=[
            pl.BlockSpec((1, TM, D_MODEL), lambda b, t: (b, t, 0)),
            pl.BlockSpec((1, 6, D_MODEL), lambda b, t: (b, 0, 0)),
            pl.BlockSpec((n_chunks, D_MODEL, n_parts * FC), const3),
            pl.BlockSpec((n_chunks, CONV_WIDTH, FC), const3),
            pl.BlockSpec((n_chunks, FC, D_MODEL), const3),
            pl.BlockSpec((1, D_MODEL), lambda b, t: (0, 0)),
            pl.BlockSpec((1, D_MODEL), lambda b, t: (0, 0)),
        ],
        out_specs=pl.BlockSpec((1, TM, D_MODEL), lambda b, t: (b, t, 0)),
        out_shape=jax.ShapeDtypeStruct((BATCH, SEQ, D_MODEL), F32),
        scratch_shapes=[
            pltpu.VMEM((TM, D_MODEL), BF16),
            pltpu.VMEM((n_chunks, CONV_HALO, FC), F32),
            pltpu.VMEM((CONV_HALO + TM, FC), F32),
            pltpu.VMEM((TM, D_MODEL), F32),
        ],
        compiler_params=_cparams(2),
        name="gated_" + mode,
    )(x, mod, win_r, wconv_r, wout_r, ln_g.reshape(1, D_MODEL), ln_b.reshape(1, D_MODEL))


def _pool_kernel(x_ref, mod_ref, win_ref, wgrp_ref, scale_ref, wout_ref, lng_ref, lnb_ref,
                 o_ref, carry_ref, ubuf_ref, z_ref):
    t = pl.program_id(1)

    @pl.when(t == 0)
    def _():
        carry_ref[...] = jnp.zeros_like(carry_ref)

    xt = x_ref[0]
    m = mod_ref[0]
    sh, sc, gate = m[0:1], m[1:2], m[2:3]
    h = (xt * (1.0 + sc) + sh).astype(BF16)
    u = jnp.dot(h, win_ref[...], preferred_element_type=F32)
    ubuf_ref[0:POOL_HALO, :] = carry_ref[...]
    ubuf_ref[POOL_HALO:, :] = u
    carry_ref[...] = u[TM - POOL_HALO:, :]
    pos = t * TM + lax.broadcasted_iota(jnp.int32, (TM, 1), 0)
    for g, w in enumerate(POOL_WINDOWS):
        lo, hi = g * POOL_GROUP, (g + 1) * POOL_GROUP
        e = ubuf_ref[:, lo:hi]
        s = e
        k = 1
        while k < w:
            s = s + pltpu.roll(s, k, axis=0)
            k *= 2
        cnt = jnp.minimum(pos + 1, w).astype(F32)
        pooled = s[POOL_HALO:, :] / cnt - e[POOL_HALO:, :]
        zg = jnp.dot(pooled.astype(BF16), wgrp_ref[g], preferred_element_type=F32)
        z_ref[:, lo:hi] = (zg * scale_ref[:, lo:hi]).astype(BF16)
    y = jnp.dot(z_ref[...], wout_ref[...], preferred_element_type=F32)
    o_ref[0] = _deepnorm_ln(xt, gate, y, lng_ref[...], lnb_ref[...])


def _pool_layer(x, mod, w_in, w_grp, scale, w_out, ln_g, ln_b):
    c2 = lambda b, t: (0, 0)
    return pl.pallas_call(
        _pool_kernel,
        grid=(BATCH, SEQ // TM),
        in_specs=[
            pl.BlockSpec((1, TM, D_MODEL), lambda b, t: (b, t, 0)),
            pl.BlockSpec((1, 6, D_MODEL), lambda b, t: (b, 0, 0)),
            pl.BlockSpec((D_MODEL, D_MODEL), c2),
            pl.BlockSpec((N_POOL_GROUPS, POOL_GROUP, POOL_GROUP), lambda b, t: (0, 0, 0)),
            pl.BlockSpec((1, D_MODEL), c2),
            pl.BlockSpec((D_MODEL, D_MODEL), c2),
            pl.BlockSpec((1, D_MODEL), c2),
            pl.BlockSpec((1, D_MODEL), c2),
        ],
        out_specs=pl.BlockSpec((1, TM, D_MODEL), lambda b, t: (b, t, 0)),
        out_shape=jax.ShapeDtypeStruct((BATCH, SEQ, D_MODEL), F32),
        scratch_shapes=[
            pltpu.VMEM((POOL_HALO, D_MODEL), F32),
            pltpu.VMEM((POOL_HALO + TM, D_MODEL), F32),
            pltpu.VMEM((TM, D_MODEL), BF16),
        ],
        compiler_params=_cparams(2),
        name="pool_mixer",
    )(x, mod, w_in.astype(BF16), w_grp.astype(BF16), scale.reshape(1, D_MODEL), w_out.astype(BF16),
      ln_g.reshape(1, D_MODEL), ln_b.reshape(1, D_MODEL))


NSA_Q0, NSA_SEL0, NSA_WIN0, NSA_CMP0, NSA_G0 = 0, 1024, 1536, 2048, 2560
NSA_COLS = 2688


def _nsa_proj_kernel(x_ref, mod_ref, w_ref, q_ref, kvsel_ref, kvwin_ref, kvc_ref, gates_ref):
    xt = x_ref[0]
    m = mod_ref[0]
    sh, sc = m[0:1], m[1:2]
    h = (xt * (1.0 + sc) + sh).astype(BF16)
    r = jnp.dot(h, w_ref[...], preferred_element_type=F32)
    q_ref[0] = (r[:, NSA_Q0:NSA_SEL0] * (HEAD_DIM ** -0.5)).astype(BF16)
    kvsel_ref[0] = r[:, NSA_SEL0:NSA_WIN0].astype(BF16)
    kvwin_ref[0] = r[:, NSA_WIN0:NSA_CMP0].astype(BF16)
    kvc_ref[0] = r[:, NSA_CMP0:NSA_G0]
    gates_ref[0] = jax.nn.sigmoid(r[:, NSA_G0:NSA_COLS])


def _nsa_proj(x, mod, w_in):
    q0 = 0
    kc0 = D_MODEL
    vc0, ks0, vs0, kw0, vw0, gl0 = (kc0 + i * KV_WIDTH for i in range(1, 7))

    def per_group(k0, v0):
        cols = []
        for g in range(N_KV_GROUPS):
            cols.append(jnp.arange(k0 + g * HEAD_DIM, k0 + (g + 1) * HEAD_DIM))
            cols.append(jnp.arange(v0 + g * HEAD_DIM, v0 + (g + 1) * HEAD_DIM))
        return cols

    cols = jnp.concatenate([jnp.arange(q0, D_MODEL)] + per_group(ks0, vs0) + per_group(kw0, vw0)
                           + [jnp.arange(kc0, ks0), jnp.arange(gl0, gl0 + N_GATES)])
    w = jnp.take(w_in, cols, axis=1)
    w = jnp.pad(w, ((0, 0), (0, NSA_COLS - w.shape[1]))).astype(BF16)
    row = lambda b, t: (b, t, 0)
    return pl.pallas_call(
        _nsa_proj_kernel,
        grid=(BATCH, SEQ // TM),
        in_specs=[
            pl.BlockSpec((1, TM, D_MODEL), row),
            pl.BlockSpec((1, 6, D_MODEL), lambda b, t: (b, 0, 0)),
            pl.BlockSpec((D_MODEL, NSA_COLS), lambda b, t: (0, 0)),
        ],
        out_specs=[
            pl.BlockSpec((1, TM, D_MODEL), row),
            pl.BlockSpec((1, TM, 2 * KV_WIDTH), row),
            pl.BlockSpec((1, TM, 2 * KV_WIDTH), row),
            pl.BlockSpec((1, TM, 2 * KV_WIDTH), row),
            pl.BlockSpec((1, TM, LANES), row),
        ],
        out_shape=[
            jax.ShapeDtypeStruct((BATCH, SEQ, D_MODEL), BF16),
            jax.ShapeDtypeStruct((BATCH, SEQ, 2 * KV_WIDTH), BF16),
            jax.ShapeDtypeStruct((BATCH, SEQ, 2 * KV_WIDTH), BF16),
            jax.ShapeDtypeStruct((BATCH, SEQ, 2 * KV_WIDTH), F32),
            jax.ShapeDtypeStruct((BATCH, SEQ, LANES), F32),
        ],
        compiler_params=_cparams(2),
        name="nsa_proj",
    )(x, mod, w)


CHUNK_FEAT = CMP_STRIDE * HEAD_DIM


def _cmp_kernel(x_ref, posa_ref, posb_ref, w1a_ref, w1b_ref, w2_ref, o_ref):
    x = x_ref[0, 0]
    xa = (x + posa_ref[0]).astype(BF16)
    xb = (x + posb_ref[0]).astype(BF16)
    a = jnp.dot(xa, w1a_ref[0], preferred_element_type=F32)
    b = jnp.dot(xb, w1b_ref[0], preferred_element_type=F32)
    hid = _gelu(a + pltpu.roll(b, N_CMP_PAD - 1, axis=0))
    out = jnp.dot(hid.astype(BF16), w2_ref[0], preferred_element_type=F32)
    n = lax.broadcasted_iota(jnp.int32, out.shape, 0)
    o_ref[0, 0] = jnp.where(n < N_CMP, out, 0.0).astype(BF16)


def _compress(kvc, pos_k, pos_v, k_w1, k_w2, v_w1, v_w2):
    n_chunk = SEQ // CMP_STRIDE
    x = (kvc.reshape(BATCH, n_chunk, CMP_STRIDE, 2 * N_KV_GROUPS, HEAD_DIM)
         .transpose(0, 3, 1, 2, 4).reshape(BATCH, 2 * N_KV_GROUPS, n_chunk, CHUNK_FEAT))
    pos = jnp.stack([pos_k, pos_v])
    posa = pos[:, :CMP_STRIDE].reshape(2, 1, CHUNK_FEAT)
    posb = pos[:, CMP_STRIDE:].reshape(2, 1, CHUNK_FEAT)
    w1 = jnp.stack([k_w1, v_w1]).astype(BF16)
    w1a = w1[:, :CMP_STRIDE].reshape(2, CHUNK_FEAT, CMP_HIDDEN)
    w1b = w1[:, CMP_STRIDE:].reshape(2, CHUNK_FEAT, CMP_HIDDEN)
    w2 = jnp.tile(jnp.stack([k_w2, v_w2]).astype(BF16), (1, 1, HEADS_PER_GROUP))
    which = lambda b, i: (i // N_KV_GROUPS, 0, 0)
    return pl.pallas_call(
        _cmp_kernel,
        grid=(BATCH, 2 * N_KV_GROUPS),
        in_specs=[
            pl.BlockSpec((1, 1, n_chunk, CHUNK_FEAT), lambda b, i: (b, i, 0, 0)),
            pl.BlockSpec((1, 1, CHUNK_FEAT), which),
            pl.BlockSpec((1, 1, CHUNK_FEAT), which),
            pl.BlockSpec((1, CHUNK_FEAT, CMP_HIDDEN), which),
            pl.BlockSpec((1, CHUNK_FEAT, CMP_HIDDEN), which),
            pl.BlockSpec((1, CMP_HIDDEN, HEADS_PER_GROUP * HEAD_DIM), which),
        ],
        out_specs=pl.BlockSpec((1, 1, N_CMP_PAD, HEADS_PER_GROUP * HEAD_DIM), lambda b, i: (b, i, 0, 0)),
        out_shape=jax.ShapeDtypeStruct((BATCH, 2 * N_KV_GROUPS, N_CMP_PAD, HEADS_PER_GROUP * HEAD_DIM), BF16),
        compiler_params=_cparams(2),
        name="nsa_compress",
    )(x, posa, posb, w1a, w1b, w2)


def _qk(a, b):
    return lax.dot_general(a, b, (((1,), (1,)), ((), ())), preferred_element_type=F32)


def _attn_kernel(q_ref, kvsel_ref, kvwin_ref, kc_ref, vc_ref, gates_ref, rk_ref, rv_ref, ovl_ref, e_ref, gx_ref,
                 o_ref, k4s_ref, v4s_ref, k4w_ref, v4w_ref, m_ref, l_ref, acc_ref):
    g = pl.program_id(1)
    qi = pl.program_id(2)
    q0 = qi * TQ

    @pl.when(qi == 0)
    def _():
        rows = 512

        def fill(i, carry):
            r0 = pl.multiple_of(i * rows, rows)
            for src, kdst, vdst in ((kvsel_ref, k4s_ref, v4s_ref), (kvwin_ref, k4w_ref, v4w_ref)):
                kv = src[0, pl.ds(r0, rows), :]
                kdst[pl.ds(r0, rows), :] = jnp.dot(kv, rk_ref[...], preferred_element_type=F32).astype(BF16)
                vdst[pl.ds(r0, rows), :] = jnp.dot(kv, rv_ref[...], preferred_element_type=F32).astype(BF16)
            return carry

        lax.fori_loop(0, SEQ // rows, fill, 0)

    qf = q_ref[0].astype(F32)
    head_of_lane = lax.broadcasted_iota(jnp.int32, (TQ, HEADS_PER_GROUP * HEAD_DIM), 1) >> _log2(HEAD_DIM)
    qst = jnp.concatenate([jnp.where(head_of_lane == r, qf, 0.0) for r in range(HEADS_PER_GROUP)],
                          axis=0).astype(BF16)
    row = lax.broadcasted_iota(jnp.int32, (RQ, 1), 0)
    head_r = row >> _log2(TQ)
    tq = q0 + (row - head_r * TQ)
    tq_f = tq.astype(F32)
    hh = (g * HEADS_PER_GROUP + head_r + 1).astype(F32)
    slope = jnp.exp2(-8.0 * hh / N_HEADS)

    def unstack(a):
        out = jnp.zeros((TQ, HEADS_PER_GROUP * HEAD_DIM), F32)
        for r in range(HEADS_PER_GROUP):
            out = jnp.where(head_of_lane == r, a[r * TQ:(r + 1) * TQ, :], out)
        return out

    n_idx = lax.broadcasted_iota(jnp.int32, (1, N_CMP_PAD), 1)
    cmp_end = n_idx * CMP_STRIDE + (CMP_BLOCK - 1)
    centre = (n_idx * CMP_STRIDE).astype(F32) + 0.5 * (CMP_BLOCK - 1)
    s = _qk(qst, kc_ref[0, 0])
    valid = (cmp_end <= tq) & (n_idx < N_CMP)
    s = jnp.where(valid, s - slope * (tq_f - centre), NEG_INF)
    mx = jnp.max(s, axis=-1, keepdims=True)
    p = jnp.exp(s - mx)
    p = p / jnp.sum(p, axis=-1, keepdims=True)
    p = p * (tq >= CMP_BLOCK - 1).astype(F32)
    o_c = unstack(jnp.dot(p.astype(BF16), vc_ref[0, 0], preferred_element_type=F32))

    psum = p[0:TQ] + p[TQ:2 * TQ] + p[2 * TQ:3 * TQ] + p[3 * TQ:4 * TQ]
    imp_t = _dot_sel(psum, ovl_ref[...]).T[0:N_SEL_BLOCKS, :]
    jb = lax.broadcasted_iota(jnp.int32, (N_SEL_BLOCKS, TQ), 0)
    bt = (q0 + lax.broadcasted_iota(jnp.int32, (N_SEL_BLOCKS, TQ), 1)) >> _log2(SEL_BLOCK)
    forced = (jb == 0) | (jb == bt) | (jb == bt - 1)
    score = jnp.where(forced, FORCE_SCORE, jnp.where(jb <= bt, imp_t, -FORCE_SCORE))
    rank = jnp.zeros((N_SEL_BLOCKS, TQ), jnp.int32)
    for j2 in range(N_SEL_BLOCKS):
        other = score[j2:j2 + 1, :]
        beats = (other > score) | ((other == score) & (jb > j2))
        rank = rank + beats.astype(jnp.int32)
    sel_bias_t = jnp.where(rank < N_SELECT, 0.0, NEG_INF)
    sel_bias = jnp.concatenate([sel_bias_t, jnp.zeros_like(sel_bias_t)], axis=0).T.astype(BF16)

    def attend(k4_ref, v4_ref, tk, kt_lo, kt_hi, extra_bias, window):
        m_ref[...] = jnp.full_like(m_ref, NEG_INF)
        l_ref[...] = jnp.zeros_like(l_ref)
        acc_ref[...] = jnp.zeros_like(acc_ref)

        def body(kt, carry):
            k0 = pl.multiple_of(kt * tk, tk)
            sc = _qk(qst, k4_ref[pl.ds(k0, tk), :])
            dist = tq - (k0 + lax.broadcasted_iota(jnp.int32, (1, tk), 1))
            sc = sc - slope * dist.astype(F32)
            if extra_bias is not None:
                sc = sc + extra_bias(kt)
            ok = dist >= 0
            if window:
                ok = ok & (dist < WINDOW)
            sc = jnp.where(ok, sc, NEG_INF)
            m_old = m_ref[...]
            m_new = jnp.maximum(m_old, jnp.max(sc, axis=-1, keepdims=True))
            alpha = jnp.exp(m_old - m_new)
            pt = jnp.exp(sc - m_new)
            l_ref[...] = alpha * l_ref[...] + jnp.sum(pt, axis=-1, keepdims=True)
            acc_ref[...] = alpha * acc_ref[...] + jnp.dot(pt.astype(BF16), v4_ref[pl.ds(k0, tk), :],
                                                          preferred_element_type=F32)
            m_ref[...] = m_new
            return carry

        lax.fori_loop(kt_lo, kt_hi, body, 0)
        return unstack(acc_ref[...] / l_ref[...])

    def sel_extra(kt):
        b = jnp.dot(sel_bias, e_ref[kt], preferred_element_type=F32)
        return jnp.concatenate([b] * HEADS_PER_GROUP, axis=0)

    o_s = attend(k4s_ref, v4s_ref, TKS, 0, (q0 + TQ + TKS - 1) >> _log2(TKS), sel_extra, False)
    kt_hi = (q0 + TQ) >> _log2(TKW)
    o_w = attend(k4w_ref, v4w_ref, TKW, jnp.maximum(kt_hi - (WINDOW // TKW + 1), 0), kt_hi, None, True)

    gates = gates_ref[0]
    out = _dot_sel(gates, gx_ref[0, 0]) * o_c
    out += _dot_sel(gates, gx_ref[0, 1]) * o_s
    out += _dot_sel(gates, gx_ref[0, 2]) * o_w
    o_ref[0] = out.astype(BF16)


def _attention(q, kvsel, kvwin, kvcmp, gates):
    hw = HEADS_PER_GROUP * HEAD_DIM
    lane = jnp.arange(hw)
    src = jnp.arange(LANES)
    rk = ((src[:, None] < HEAD_DIM) & (src[:, None] == lane[None, :] % HEAD_DIM)).astype(BF16)
    rv = ((src[:, None] >= HEAD_DIM) & (src[:, None] - HEAD_DIM == lane[None, :] % HEAD_DIM)).astype(BF16)
    cmp_start = jnp.arange(N_CMP_PAD) * CMP_STRIDE
    sel_start = jnp.arange(LANES) * SEL_BLOCK
    ovl = ((cmp_start[:, None] <= sel_start[None, :] + SEL_BLOCK - 1)
           & (cmp_start[:, None] + CMP_BLOCK - 1 >= sel_start[None, :])
           & (jnp.arange(N_CMP_PAD)[:, None] < N_CMP) & (jnp.arange(LANES)[None, :] < N_SEL_BLOCKS)).astype(BF16)
    key = jnp.arange(SEQ)
    e = (jnp.arange(LANES)[:, None] == key[None, :] // SEL_BLOCK).astype(BF16)
    e = e.reshape(LANES, SEQ // TKS, TKS).transpose(1, 0, 2)
    gcol = ((jnp.arange(N_KV_GROUPS)[:, None, None] * HEADS_PER_GROUP + (lane // HEAD_DIM)[None, None, :]) * N_BRANCH
            + jnp.arange(N_BRANCH)[None, :, None])
    gx = (src[None, None, :, None] == gcol[:, :, None, :]).astype(BF16)

    nq = SEQ // TQ
    return pl.pallas_call(
        _attn_kernel,
        grid=(BATCH, N_KV_GROUPS, nq),
        in_specs=[
            pl.BlockSpec((1, TQ, hw), lambda b, g, i: (b, i, g)),
            pl.BlockSpec((1, SEQ, LANES), lambda b, g, i: (b, 0, g)),
            pl.BlockSpec((1, SEQ, LANES), lambda b, g, i: (b, 0, g)),
            pl.BlockSpec((1, 1, N_CMP_PAD, hw), lambda b, g, i: (b, g, 0, 0)),
            pl.BlockSpec((1, 1, N_CMP_PAD, hw), lambda b, g, i: (b, N_KV_GROUPS + g, 0, 0)),
            pl.BlockSpec((1, TQ, LANES), lambda b, g, i: (b, i, 0)),
            pl.BlockSpec((LANES, hw), lambda b, g, i: (0, 0)),
            pl.BlockSpec((LANES, hw), lambda b, g, i: (0, 0)),
            pl.BlockSpec((N_CMP_PAD, LANES), lambda b, g, i: (0, 0)),
            pl.BlockSpec((SEQ // TKS, LANES, TKS), lambda b, g, i: (0, 0, 0)),
            pl.BlockSpec((1, N_BRANCH, LANES, hw), lambda b, g, i: (g, 0, 0, 0)),
        ],
        out_specs=pl.BlockSpec((1, TQ, hw), lambda b, g, i: (b, i, g)),
        out_shape=jax.ShapeDtypeStruct((BATCH, SEQ, D_MODEL), BF16),
        scratch_shapes=[
            pltpu.VMEM((SEQ, hw), BF16),
            pltpu.VMEM((SEQ, hw), BF16),
            pltpu.VMEM((SEQ, hw), BF16),
            pltpu.VMEM((SEQ, hw), BF16),
            pltpu.VMEM((RQ, 1), F32),
            pltpu.VMEM((RQ, 1), F32),
            pltpu.VMEM((RQ, hw), F32),
        ],
        compiler_params=_cparams(3),
        name="nsa_attention",
    )(q, kvsel, kvwin, kvcmp, kvcmp, gates, rk, rv, ovl, e, gx)


def _out_kernel(o_ref, x_ref, mod_ref, w_ref, lng_ref, lnb_ref, y_ref):
    y = jnp.dot(o_ref[0], w_ref[...], preferred_element_type=F32)
    gate = mod_ref[0][2:3]
    y_ref[0] = _deepnorm_ln(x_ref[0], gate, y, lng_ref[...], lnb_ref[...])


def _out_proj(o, x, mod, w_out, ln_g, ln_b):
    row = lambda b, t: (b, t, 0)
    c2 = lambda b, t: (0, 0)
    return pl.pallas_call(
        _out_kernel,
        grid=(BATCH, SEQ // TM),
        in_specs=[
            pl.BlockSpec((1, TM, D_MODEL), row),
            pl.BlockSpec((1, TM, D_MODEL), row),
            pl.BlockSpec((1, 6, D_MODEL), lambda b, t: (b, 0, 0)),
            pl.BlockSpec((D_MODEL, D_MODEL), c2),
            pl.BlockSpec((1, D_MODEL), c2),
            pl.BlockSpec((1, D_MODEL), c2),
        ],
        out_specs=pl.BlockSpec((1, TM, D_MODEL), row),
        out_shape=jax.ShapeDtypeStruct((BATCH, SEQ, D_MODEL), F32),
        compiler_params=_cparams(2),
        name="nsa_out_proj",
    )(o, x, mod, w_out.astype(BF16), ln_g.reshape(1, D_MODEL), ln_b.reshape(1, D_MODEL))


def _nsa_layer(x, mod, w_in, pos_k, pos_v, k_w1, k_w2, v_w1, v_w2, w_out, ln_g, ln_b):
    q, kvsel, kvwin, kvc, gates = _nsa_proj(x, mod, w_in)
    kvcmp = _compress(kvc, pos_k, pos_v, k_w1, k_w2, v_w1, v_w2)
    o = _attention(q, kvsel, kvwin, kvcmp, gates)
    return _out_proj(o, x, mod, w_out, ln_g, ln_b)


def kernel(x, c, ada_w, ada_b, ln1_g, ln1_b, ln2_g, ln2_b, ffn_w_in, ffn_conv, ffn_w_out, conv_w_in, conv_w,
           conv_w_out, pool_w_in, pool_w_grp, pool_scale, pool_w_out, nsa_w_in, nsa_cmp_pos_k, nsa_cmp_pos_v,
           nsa_cmp_k_w1, nsa_cmp_k_w2, nsa_cmp_v_w1, nsa_cmp_v_w2, nsa_w_out):
    assert x.shape == (BATCH, SEQ, D_MODEL) and x.dtype == F32
    mods = _modulation(c, ada_w, ada_b).reshape(DEPTH, BATCH, 6, D_MODEL)
    for i in range(DEPTH):
        mod = mods[i]
        m, j = i % N_MIXERS, i // N_MIXERS
        if m == 0:
            x = _gated_layer("conv", x, mod, 0, conv_w_in[j], conv_w[j], conv_w_out[j], ln1_g[i], ln1_b[i])
        elif m == 1:
            x = _pool_layer(x, mod, pool_w_in[j], pool_w_grp[j], pool_scale[j], pool_w_out[j], ln1_g[i], ln1_b[i])
        else:
            x = _nsa_layer(x, mod, nsa_w_in[j], nsa_cmp_pos_k[j], nsa_cmp_pos_v[j], nsa_cmp_k_w1[j],
                           nsa_cmp_k_w2[j], nsa_cmp_v_w1[j], nsa_cmp_v_w2[j], nsa_w_out[j], ln1_g[i], ln1_b[i])
        x = _gated_layer("ffn", x, mod, 3, ffn_w_in[i], ffn_conv[i], ffn_w_out[i], ln2_g[i], ln2_b[i])
    return x
```

```python
import functools
import math

import jax
import jax.numpy as jnp
from jax import lax
from jax.experimental import pallas as pl
from jax.experimental.pallas import tpu as pltpu

F32 = jnp.float32
BF16 = jnp.bfloat16

D_MODEL = 1024
BATCH = 16
SEQ = 4096
DEPTH = 4
N_MIXERS = 3
CONV_WIDTH = 3
D_FF = 2816
POOL_WINDOWS = (2, 4, 8, 16)
N_POOL_GROUPS = 4
POOL_GROUP = D_MODEL // N_POOL_GROUPS
N_HEADS = 16
HEAD_DIM = 64
N_KV_GROUPS = 4
HEADS_PER_GROUP = 4
KV_WIDTH = N_KV_GROUPS * HEAD_DIM
CMP_BLOCK = 32
CMP_STRIDE = 16
CMP_HIDDEN = 256
SEL_BLOCK = 64
N_SELECT = 16
WINDOW = 512
N_BRANCH = 3
ALPHA = (2.0 * DEPTH) ** 0.25
LN_EPS = 1e-5
NEG_INF = -1e30
FORCE_SCORE = 1e4
LOG2E = math.log2(math.e)

N_CMP = (SEQ - CMP_BLOCK) // CMP_STRIDE + 1
N_CMP_PAD = 256
N_SEL_BLOCKS = SEQ // SEL_BLOCK
LANES = 128
SUBLANES = 8

TM = 512
FC = 256
CONV_HALO = 8
POOL_HALO = 16
TQ = 128
HQ = HEADS_PER_GROUP * TQ
TKS = 256
TKD = 128
VMEM_LIMIT = 56 * 1024 * 1024

AUG = 256
AUG_SEL0 = HEAD_DIM
AUG_POS0 = 2 * HEAD_DIM
AUG_POS_ROWS = 16
V_ROWS = 80
GATE_ROWS = 16


def _cparams(n_axes):
    return pltpu.CompilerParams(dimension_semantics=("arbitrary",) * n_axes,
                                vmem_limit_bytes=VMEM_LIMIT)


def _log2(n):
    assert n & (n - 1) == 0
    return n.bit_length() - 1


def _gelu(x):
    c = math.sqrt(2.0 / math.pi)
    return x * (0.5 * (1.0 + jnp.tanh(c * (x + 0.044715 * (x * x * x)))))


def _deepnorm_ln(x, gate, y, ln_g, ln_b):
    r = ALPHA * x + (1.0 + gate) * y
    mu = jnp.mean(r, axis=-1, keepdims=True)
    d = r - mu
    var = jnp.mean(d * d, axis=-1, keepdims=True)
    return d * lax.rsqrt(var + LN_EPS) * ln_g + ln_b


def _split3(a):
    a1 = a.astype(BF16)
    r1 = a - a1.astype(F32)
    a2 = r1.astype(BF16)
    a3 = (r1 - a2.astype(F32)).astype(BF16)
    return a1, a2, a3


def _ada_kernel(c_ref, w_ref, b_ref, o_ref):
    c = c_ref[...]
    cond = c * jax.nn.sigmoid(c)
    c1, c2, c3 = _split3(cond)
    w1, w2, w3 = _split3(w_ref[0])
    acc = jnp.dot(c1, w1, preferred_element_type=F32)
    acc += jnp.dot(c1, w2, preferred_element_type=F32)
    acc += jnp.dot(c2, w1, preferred_element_type=F32)
    acc += jnp.dot(c1, w3, preferred_element_type=F32)
    acc += jnp.dot(c2, w2, preferred_element_type=F32)
    acc += jnp.dot(c3, w1, preferred_element_type=F32)
    o_ref[0] = acc + b_ref[0]


def _modulation(c, ada_w, ada_b):
    nc = 1536
    n_col = 6 * D_MODEL // nc
    return pl.pallas_call(
        _ada_kernel,
        grid=(DEPTH, n_col),
        in_specs=[
            pl.BlockSpec((BATCH, D_MODEL), lambda i, j: (0, 0)),
            pl.BlockSpec((1, D_MODEL, nc), lambda i, j: (i, 0, j)),
            pl.BlockSpec((1, 1, nc), lambda i, j: (i, 0, j)),
        ],
        out_specs=pl.BlockSpec((1, BATCH, nc), lambda i, j: (i, 0, j)),
        out_shape=jax.ShapeDtypeStruct((DEPTH, BATCH, 6 * D_MODEL), F32),
        compiler_params=_cparams(2),
        name="modulation",
    )(c, ada_w, ada_b.reshape(DEPTH, 1, 6 * D_MODEL))


def _gated_kernel(mode, n_chunks, mrow, x_ref, mod_ref, win_ref, wconv_ref, wout_ref, lng_ref, lnb_ref,
                  o_ref, h_ref, carry_ref, cbuf_ref, acc_ref):
    t = pl.program_id(1)

    @pl.when(t == 0)
    def _():
        carry_ref[...] = jnp.zeros_like(carry_ref)

    xt = x_ref[0]
    m = mod_ref[0]
    sh, sc, gate = m[mrow:mrow + 1], m[mrow + 1:mrow + 2], m[mrow + 2:mrow + 3]
    h_ref[...] = (xt * (1.0 + sc) + sh).astype(BF16)
    acc_ref[...] = jnp.zeros_like(acc_ref)

    def body(j, carry):
        p = jnp.dot(h_ref[...], win_ref[j], preferred_element_type=F32)
        if mode == "conv":
            gate_in = p[:, :FC]
            cin = p[:, FC:2 * FC] * p[:, 2 * FC:]
        else:
            cin = p[:, :FC]
            gate_in = p[:, FC:]
        cbuf_ref[0:CONV_HALO, :] = carry_ref[j]
        cbuf_ref[CONV_HALO:, :] = cin
        carry_ref[j] = cin[TM - CONV_HALO:, :]
        w = wconv_ref[j]
        y = (w[0:1] * cbuf_ref[CONV_HALO - 2:CONV_HALO - 2 + TM, :]
             + w[1:2] * cbuf_ref[CONV_HALO - 1:CONV_HALO - 1 + TM, :]
             + w[2:3] * cin)
        if mode == "conv":
            z = gate_in * y
        else:
            z = _gelu(y) * gate_in
        acc_ref[...] += jnp.dot(z.astype(BF16), wout_ref[j], preferred_element_type=F32)
        return carry

    lax.fori_loop(0, n_chunks, body, 0)
    o_ref[0] = _deepnorm_ln(xt, gate, acc_ref[...], lng_ref[...], lnb_ref[...])


def _gated_layer(mode, x, mod, mrow, w_in, w_conv, w_out, ln_g, ln_b):
    feat = w_out.shape[0]
    n_parts = w_in.shape[1] // feat
    n_chunks = feat // FC
    win_r = (w_in.astype(BF16).reshape(D_MODEL, n_parts, n_chunks, FC)
             .transpose(2, 0, 1, 3).reshape(n_chunks, D_MODEL, n_parts * FC))
    wout_r = w_out.astype(BF16).reshape(n_chunks, FC, D_MODEL)
    wconv_r = w_conv.reshape(CONV_WIDTH, n_chunks, FC).transpose(1, 0, 2)
    const3 = lambda b, t: (0, 0, 0)
    return pl.pallas_call(
        functools.partial(_gated_kernel, mode, n_chunks, mrow),
        grid=(BATCH, SEQ // TM),
        in_specs=[
            pl.BlockSpec((1, TM, D_MODEL), lambda b, t: (b, t, 0)),
            pl.BlockSpec((1, 6, D_MODEL), lambda b, t: (b, 0, 0)),
            pl.BlockSpec((n_chunks, D_MODEL, n_parts * FC), const3),
            pl.BlockSpec((n_chunks, CONV_WIDTH, FC), const3),
            pl.BlockSpec((n_chunks, FC, D_MODEL), const3),
            pl.BlockSpec((1, D_MODEL), lambda b, t: (0, 0)),
            pl.BlockSpec((1, D_MODEL), lambda b, t: (0, 0)),
        ],
        out_specs=pl.BlockSpec((1, TM, D_MODEL), lambda b, t: (b, t, 0)),
        out_shape=jax.ShapeDtypeStruct((BATCH, SEQ, D_MODEL), F32),
        scratch_shapes=[
            pltpu.VMEM((TM, D_MODEL), BF16),
            pltpu.VMEM((n_chunks, CONV_HALO, FC), F32),
            pltpu.VMEM((CONV_HALO + TM, FC), F32),
            pltpu.VMEM((TM, D_MODEL), F32),
        ],
        compiler_params=_cparams(2),
        name="gated_" + mode,
    )(x, mod, win_r, wconv_r, wout_r, ln_g.reshape(1, D_MODEL), ln_b.reshape(1, D_MODEL))


def _pool_kernel(x_ref, mod_ref, win_ref, wgrp_ref, scale_ref, wout_ref, lng_ref, lnb_ref,
                 o_ref, carry_ref, ubuf_ref, z_ref):
    t = pl.program_id(1)

    @pl.when(t == 0)
    def _():
        carry_ref[...] = jnp.zeros_like(carry_ref)

    xt = x_ref[0]
    m = mod_ref[0]
    sh, sc, gate = m[0:1], m[1:2], m[2:3]
    h = (xt * (1.0 + sc) + sh).astype(BF16)
    u = jnp.dot(h, win_ref[...], preferred_element_type=F32)
    ubuf_ref[0:POOL_HALO, :] = carry_ref[...]
    ubuf_ref[POOL_HALO:, :] = u
    carry_ref[...] = u[TM - POOL_HALO:, :]
    pos = t * TM + lax.broadcasted_iota(jnp.int32, (TM, 1), 0)
    for g, w in enumerate(POOL_WINDOWS):
        lo, hi = g * POOL_GROUP, (g + 1) * POOL_GROUP
        e = ubuf_ref[:, lo:hi]
        s = e
        k = 1
        while k < w:
            s = s + pltpu.roll(s, k, axis=0)
            k *= 2
        cnt = jnp.minimum(pos + 1, w).astype(F32)
        pooled = s[POOL_HALO:, :] / cnt - e[POOL_HALO:, :]
        zg = jnp.dot(pooled.astype(BF16), wgrp_ref[g], preferred_element_type=F32)
        z_ref[:, lo:hi] = (zg * scale_ref[:, lo:hi]).astype(BF16)
    y = jnp.dot(z_ref[...], wout_ref[...], preferred_element_type=F32)
    o_ref[0] = _deepnorm_ln(xt, gate, y, lng_ref[...], lnb_ref[...])


def _pool_layer(x, mod, w_in, w_grp, scale, w_out, ln_g, ln_b):
    c2 = lambda b, t: (0, 0)
    return pl.pallas_call(
        _pool_kernel,
        grid=(BATCH, SEQ // TM),
        in_specs=[
            pl.BlockSpec((1, TM, D_MODEL), lambda b, t: (b, t, 0)),
            pl.BlockSpec((1, 6, D_MODEL), lambda b, t: (b, 0, 0)),
            pl.BlockSpec((D_MODEL, D_MODEL), c2),
            pl.BlockSpec((N_POOL_GROUPS, POOL_GROUP, POOL_GROUP), lambda b, t: (0, 0, 0)),
            pl.BlockSpec((1, D_MODEL), c2),
            pl.BlockSpec((D_MODEL, D_MODEL), c2),
            pl.BlockSpec((1, D_MODEL), c2),
            pl.BlockSpec((1, D_MODEL), c2),
        ],
        out_specs=pl.BlockSpec((1, TM, D_MODEL), lambda b, t: (b, t, 0)),
        out_shape=jax.ShapeDtypeStruct((BATCH, SEQ, D_MODEL), F32),
        scratch_shapes=[
            pltpu.VMEM((POOL_HALO, D_MODEL), F32),
            pltpu.VMEM((POOL_HALO + TM, D_MODEL), F32),
            pltpu.VMEM((TM, D_MODEL), BF16),
        ],
        compiler_params=_cparams(2),
        name="pool_mixer",
    )(x, mod, w_in.astype(BF16), w_grp.astype(BF16), scale.reshape(1, D_MODEL), w_out.astype(BF16),
      ln_g.reshape(1, D_MODEL), ln_b.reshape(1, D_MODEL))


NSA_T_ROWS = D_MODEL + 2 * KV_WIDTH + N_KV_GROUPS * GATE_ROWS
Q_SCALE = HEAD_DIM ** -0.5 * LOG2E


def _nsa_proj_kernel(x_ref, mod_ref, wn_ref, wt_ref, qt_ref, kk_ref, vt_ref, gt_ref, kvc_ref):
    xt = x_ref[0]
    m = mod_ref[0]
    sh, sc = m[0:1], m[1:2]
    h = (xt * (1.0 + sc) + sh).astype(BF16)
    rn = jnp.dot(h, wn_ref[...], preferred_element_type=F32)
    kk_ref[0] = rn[:, :2 * KV_WIDTH].astype(BF16)
    kvc_ref[0] = rn[:, 2 * KV_WIDTH:]
    rt = lax.dot_general(wt_ref[...], h, (((1,), (1,)), ((), ())), preferred_element_type=F32)
    qt_ref[0] = (rt[0:D_MODEL] * Q_SCALE).astype(BF16)
    v_t = rt[D_MODEL:D_MODEL + 2 * KV_WIDTH].astype(BF16)
    for j in range(TM // LANES):
        vt_ref[0, j] = v_t[:, j * LANES:(j + 1) * LANES]
    gt_ref[0] = jax.nn.sigmoid(rt[D_MODEL + 2 * KV_WIDTH:])


def _nsa_proj(x, mod, w_in):
    kc0 = D_MODEL
    vc0, ks0, vs0, kw0, vw0, gl0 = (kc0 + i * KV_WIDTH for i in range(1, 7))
    grp = lambda c0, g: jnp.arange(c0 + g * HEAD_DIM, c0 + (g + 1) * HEAD_DIM)
    cols_n = jnp.concatenate([jnp.concatenate([grp(ks0, g), grp(kw0, g)]) for g in range(N_KV_GROUPS)]
                             + [jnp.arange(kc0, ks0)])
    wn = jnp.take(w_in, cols_n, axis=1).astype(BF16)
    cols_v = jnp.concatenate([jnp.concatenate([grp(vs0, g), grp(vw0, g)]) for g in range(N_KV_GROUPS)])
    n_gate = HEADS_PER_GROUP * N_BRANCH
    w_gate = w_in[:, gl0:gl0 + N_KV_GROUPS * n_gate].reshape(D_MODEL, N_KV_GROUPS, n_gate)
    w_gate = jnp.pad(w_gate, ((0, 0), (0, 0), (0, GATE_ROWS - n_gate))).reshape(D_MODEL, N_KV_GROUPS * GATE_ROWS)
    wt = jnp.concatenate([w_in[:, :D_MODEL], jnp.take(w_in, cols_v, axis=1), w_gate], axis=1).T.astype(BF16)
    row = lambda b, t: (b, t, 0)
    colt = lambda b, t: (b, 0, t)
    return pl.pallas_call(
        _nsa_proj_kernel,
        grid=(BATCH, SEQ // TM),
        in_specs=[
            pl.BlockSpec((1, TM, D_MODEL), row),
            pl.BlockSpec((1, 6, D_MODEL), lambda b, t: (b, 0, 0)),
            pl.BlockSpec((D_MODEL, 4 * KV_WIDTH), lambda b, t: (0, 0)),
            pl.BlockSpec((NSA_T_ROWS, D_MODEL), lambda b, t: (0, 0)),
        ],
        out_specs=[
            pl.BlockSpec((1, D_MODEL, TM), colt),
            pl.BlockSpec((1, TM, 2 * KV_WIDTH), row),
            pl.BlockSpec((1, TM // LANES, 2 * KV_WIDTH, LANES), lambda b, t: (b, t, 0, 0)),
            pl.BlockSpec((1, N_KV_GROUPS * GATE_ROWS, TM), colt),
            pl.BlockSpec((1, TM, 2 * KV_WIDTH), row),
        ],
        out_shape=[
            jax.ShapeDtypeStruct((BATCH, D_MODEL, SEQ), BF16),
            jax.ShapeDtypeStruct((BATCH, SEQ, 2 * KV_WIDTH), BF16),
            jax.ShapeDtypeStruct((BATCH, SEQ // LANES, 2 * KV_WIDTH, LANES), BF16),
            jax.ShapeDtypeStruct((BATCH, N_KV_GROUPS * GATE_ROWS, SEQ), F32),
            jax.ShapeDtypeStruct((BATCH, SEQ, 2 * KV_WIDTH), F32),
        ],
        compiler_params=_cparams(2),
        name="nsa_proj",
    )(x, mod, wn, wt)


CHUNK_FEAT = CMP_STRIDE * HEAD_DIM


def _cmp_hidden(x_ref, posa_ref, posb_ref, w1a_ref, w1b_ref):
    x = x_ref[0, 0]
    xa = (x + posa_ref[0]).astype(BF16)
    xb = (x + posb_ref[0]).astype(BF16)
    a = jnp.dot(xa, w1a_ref[0], preferred_element_type=F32)
    b = jnp.dot(xb, w1b_ref[0], preferred_element_type=F32)
    return _gelu(a + pltpu.roll(b, N_CMP_PAD - 1, axis=0)).astype(BF16)


def _cmp_kernel(xk_ref, xv_ref, posa_ref, posb_ref, w1a_ref, w1b_ref, w2k_ref, w2vt_ref, posc_ref, kc_ref, vct_ref):
    hk = _cmp_hidden(xk_ref, posa_ref.at[0:1], posb_ref.at[0:1], w1a_ref.at[0:1], w1b_ref.at[0:1])
    kc_ref[0, 0] = (jnp.dot(hk, w2k_ref[...], preferred_element_type=F32) + posc_ref[...]).astype(BF16)
    hv = _cmp_hidden(xv_ref, posa_ref.at[1:2], posb_ref.at[1:2], w1a_ref.at[1:2], w1b_ref.at[1:2])
    vct = lax.dot_general(w2vt_ref[...], hv, (((1,), (1,)), ((), ())), preferred_element_type=F32)
    n = lax.broadcasted_iota(jnp.int32, vct.shape, 1)
    vct_ref[0, 0] = jnp.where(n < N_CMP, vct, 0.0).astype(BF16)


def _pos_features(pos_hi, pos_lo, n):
    feat = jnp.stack([pos_hi] * 3 + [pos_lo] * 3, axis=1).astype(F32)
    return jnp.pad(feat, ((0, 0), (AUG_POS0, AUG - AUG_POS0 - 6)))


def _compress(kvc, pos_k, pos_v, k_w1, k_w2, v_w1, v_w2):
    n_chunk = SEQ // CMP_STRIDE
    x = (kvc.reshape(BATCH, n_chunk, CMP_STRIDE, 2 * N_KV_GROUPS, HEAD_DIM)
         .transpose(0, 3, 1, 2, 4).reshape(BATCH, 2 * N_KV_GROUPS, n_chunk, CHUNK_FEAT))
    pos = jnp.stack([pos_k, pos_v])
    posa = pos[:, :CMP_STRIDE].reshape(2, 1, CHUNK_FEAT)
    posb = pos[:, CMP_STRIDE:].reshape(2, 1, CHUNK_FEAT)
    w1 = jnp.stack([k_w1, v_w1]).astype(BF16)
    w1a = w1[:, :CMP_STRIDE].reshape(2, CHUNK_FEAT, CMP_HIDDEN)
    w1b = w1[:, CMP_STRIDE:].reshape(2, CHUNK_FEAT, CMP_HIDDEN)
    w2k = jnp.pad(k_w2, ((0, 0), (0, AUG - HEAD_DIM))).astype(BF16)
    w2vt = v_w2.T.astype(BF16)
    nb = jnp.arange(N_CMP_PAD)
    posc = _pos_features(nb // 4, (nb % 4) * CMP_STRIDE + 0.5 * (CMP_BLOCK - 1), N_CMP_PAD)
    full = lambda shape: pl.BlockSpec(shape, lambda b, g: (0,) * len(shape))
    return pl.pallas_call(
        _cmp_kernel,
        grid=(BATCH, N_KV_GROUPS),
        in_specs=[
            pl.BlockSpec((1, 1, n_chunk, CHUNK_FEAT), lambda b, g: (b, g, 0, 0)),
            pl.BlockSpec((1, 1, n_chunk, CHUNK_FEAT), lambda b, g: (b, N_KV_GROUPS + g, 0, 0)),
            full((2, 1, CHUNK_FEAT)),
            full((2, 1, CHUNK_FEAT)),
            full((2, CHUNK_FEAT, CMP_HIDDEN)),
            full((2, CHUNK_FEAT, CMP_HIDDEN)),
            full((CMP_HIDDEN, AUG)),
            full((HEAD_DIM, CMP_HIDDEN)),
            full((N_CMP_PAD, AUG)),
        ],
        out_specs=[
            pl.BlockSpec((1, 1, N_CMP_PAD, AUG), lambda b, g: (b, g, 0, 0)),
            pl.BlockSpec((1, 1, HEAD_DIM, N_CMP_PAD), lambda b, g: (b, g, 0, 0)),
        ],
        out_shape=[
            jax.ShapeDtypeStruct((BATCH, N_KV_GROUPS, N_CMP_PAD, AUG), BF16),
            jax.ShapeDtypeStruct((BATCH, N_KV_GROUPS, HEAD_DIM, N_CMP_PAD), BF16),
        ],
        compiler_params=_cparams(2),
        name="nsa_compress",
    )(x, x, posa, posb, w1a, w1b, w2k, w2vt, posc)


def _attn_kernel(qt_ref, kk_ref, vt_ref, kc_ref, vct_ref, gt_ref, rs_ref, rw_ref, consts_ref, constw_ref,
                 ovlt_ref, slt_ref, cmpd_ref, tri_ref, upper_ref,
                 o_ref, ksa_ref, kwa_ref, vst_ref, vwt_ref, wq_ref, m_ref, acc_ref):
    qi = pl.program_id(2)
    q0 = qi * TQ
    n_vt = SEQ // LANES

    @pl.when(qi == 0)
    def _():
        rows = 512

        def fill(i, carry):
            r0 = pl.multiple_of(i * rows, rows)
            kk = kk_ref[0, pl.ds(r0, rows), :]
            ksa_ref[pl.ds(r0, rows), :] = (jnp.dot(kk, rs_ref[...], preferred_element_type=F32)
                                           + consts_ref[pl.ds(r0, rows), :].astype(F32)).astype(BF16)
            kwa_ref[pl.ds(r0, rows), :] = (jnp.dot(kk, rw_ref[...], preferred_element_type=F32)
                                           + constw_ref[pl.ds(r0, rows), :].astype(F32)).astype(BF16)
            return carry

        lax.fori_loop(0, SEQ // rows, fill, 0)
        ones_pad = (lax.broadcasted_iota(jnp.int32, (n_vt, V_ROWS - HEAD_DIM, LANES), 1) == 0).astype(BF16)
        vst_ref[:, 0:HEAD_DIM, :] = vt_ref[0, :, 0:HEAD_DIM, :]
        vst_ref[:, HEAD_DIM:, :] = ones_pad
        vwt_ref[:, 0:HEAD_DIM, :] = vt_ref[0, :, HEAD_DIM:, :]
        vwt_ref[:, HEAD_DIM:, :] = ones_pad
        wq_ref[AUG_POS0:AUG_POS0 + AUG_POS_ROWS, :] = slt_ref[0]
        wq_ref[AUG_POS0 + AUG_POS_ROWS:, :] = jnp.zeros((AUG - AUG_POS0 - AUG_POS_ROWS, HQ), BF16)

    qt = qt_ref[0]
    for r in range(HEADS_PER_GROUP):
        wq_ref[0:HEAD_DIM, r * TQ:(r + 1) * TQ] = qt[r * HEAD_DIM:(r + 1) * HEAD_DIM, :]
    wq_ref[AUG_SEL0:AUG_SEL0 + N_SEL_BLOCKS, :] = jnp.zeros((N_SEL_BLOCKS, HQ), BF16)

    lane_q = lax.broadcasted_iota(jnp.int32, (1, HQ), 1) & (TQ - 1)

    s = jnp.dot(kc_ref[0, 0], wq_ref[...], preferred_element_type=F32)
    s = jnp.where(cmpd_ref[...] <= q0, s, NEG_INF)
    mx = jnp.max(s, axis=0, keepdims=True)
    p = jnp.exp2(s - mx)
    has_c = (q0 + lane_q >= CMP_BLOCK - 1).astype(F32)
    p = p * (has_c / jnp.sum(p, axis=0, keepdims=True))
    oc_t = jnp.dot(vct_ref[0, 0], p.astype(BF16), preferred_element_type=F32)

    psum = p[:, 0:TQ] + p[:, TQ:2 * TQ] + p[:, 2 * TQ:3 * TQ] + p[:, 3 * TQ:4 * TQ]
    p1, p2, p3 = _split3(psum)
    ovlt = ovlt_ref[...]
    imp_t = (jnp.dot(ovlt, p1, preferred_element_type=F32) + jnp.dot(ovlt, p2, preferred_element_type=F32)
             + jnp.dot(ovlt, p3, preferred_element_type=F32))
    jb = lax.broadcasted_iota(jnp.int32, (N_SEL_BLOCKS, TQ), 0)
    bt = (q0 + lax.broadcasted_iota(jnp.int32, (N_SEL_BLOCKS, TQ), 1)) >> _log2(SEL_BLOCK)
    forced = (jb == 0) | (jb == bt) | (jb == bt - 1)
    score = jnp.where(forced, FORCE_SCORE, jnp.where(jb <= bt, imp_t, -FORCE_SCORE))
    rank = jnp.zeros((N_SEL_BLOCKS, TQ), jnp.int32)
    for j2 in range(N_SEL_BLOCKS):
        other = score[j2:j2 + 1, :]
        beats = (other > score) | ((other == score) & (jb > j2))
        rank = rank + beats.astype(jnp.int32)
    sel_bias = jnp.where(rank < N_SELECT, 0.0, NEG_INF).astype(BF16)
    for r in range(HEADS_PER_GROUP):
        wq_ref[AUG_SEL0:AUG_SEL0 + N_SEL_BLOCKS, r * TQ:(r + 1) * TQ] = sel_bias

    def reset():
        m_ref[...] = jnp.full_like(m_ref, NEG_INF)
        acc_ref[...] = jnp.zeros_like(acc_ref)

    def step(ka_ref, v_ref, k0, tk, mask_bias):
        k0 = pl.multiple_of(k0, TKD)
        sc = jnp.dot(ka_ref[pl.ds(k0, tk), :], wq_ref[...], preferred_element_type=F32)
        if mask_bias is not None:
            sc = sc + mask_bias[...]
        m_old = m_ref[...]
        m_new = jnp.maximum(m_old, jnp.max(sc, axis=0, keepdims=True))
        alpha = jnp.exp2(m_old - m_new)
        pt = jnp.exp2(sc - m_new).astype(BF16)
        kt = k0 >> _log2(LANES)
        v_t = jnp.concatenate([v_ref[kt + j] for j in range(tk // LANES)], axis=1)
        acc_ref[...] = alpha * acc_ref[...] + jnp.dot(v_t, pt, preferred_element_type=F32)
        m_ref[...] = m_new

    def result():
        acc = acc_ref[...]
        return acc[0:HEAD_DIM] / acc[HEAD_DIM:HEAD_DIM + 1]

    reset()
    n_full = q0 >> _log2(TKS)

    def sel_body(kt, carry):
        step(ksa_ref, vst_ref, kt * TKS, TKS, None)
        return carry

    lax.fori_loop(0, n_full, sel_body, 0)

    @pl.when((q0 & (TKS - 1)) != 0)
    def _():
        step(ksa_ref, vst_ref, q0 - TKD, TKD, None)

    step(ksa_ref, vst_ref, q0, TKD, tri_ref)
    os_t = result()

    reset()

    @pl.when(q0 >= WINDOW)
    def _():
        step(kwa_ref, vwt_ref, q0 - WINDOW, TKD, upper_ref)

    for j in range(WINDOW // TKD - 1, 0, -1):
        @pl.when(q0 >= j * TKD)
        def _():
            step(kwa_ref, vwt_ref, q0 - j * TKD, TKD, None)

    step(kwa_ref, vwt_ref, q0, TKD, tri_ref)
    ow_t = result()

    gt = gt_ref[0]
    blocks = []
    for r in range(HEADS_PER_GROUP):
        sl = slice(r * TQ, (r + 1) * TQ)
        blocks.append(gt[3 * r:3 * r + 1] * oc_t[:, sl] + gt[3 * r + 1:3 * r + 2] * os_t[:, sl]
                      + gt[3 * r + 2:3 * r + 3] * ow_t[:, sl])
    o_ref[0] = jnp.concatenate(blocks, axis=0).T.astype(BF16)


def _attention(qt, kk, vt, kc, vct, gt):
    hw = HEADS_PER_GROUP * HEAD_DIM
    src = jnp.arange(LANES)
    col = jnp.arange(AUG)
    rs = (src[:, None] == col[None, :]).astype(BF16) * (src[:, None] < HEAD_DIM)
    rw = (src[:, None] - HEAD_DIM == col[None, :]).astype(BF16) * (src[:, None] >= HEAD_DIM)
    key = jnp.arange(SEQ)
    pos = _pos_features(key // SEL_BLOCK, key % SEL_BLOCK, SEQ)
    onehot = (col[None, :] - AUG_SEL0 == (key // SEL_BLOCK)[:, None]).astype(F32)
    consts = (pos + onehot).astype(BF16)
    constw = pos.astype(BF16)
    cmp_start = jnp.arange(N_CMP_PAD) * CMP_STRIDE
    sel_start = jnp.arange(N_SEL_BLOCKS) * SEL_BLOCK
    ovlt = ((cmp_start[None, :] <= sel_start[:, None] + SEL_BLOCK - 1)
            & (cmp_start[None, :] + CMP_BLOCK - 1 >= sel_start[:, None])
            & (jnp.arange(N_CMP_PAD)[None, :] < N_CMP)).astype(BF16)
    hh = jnp.arange(1, N_HEADS + 1, dtype=F32)
    slope = jnp.exp2(-8.0 * hh / N_HEADS) * LOG2E
    s1, s2, s3 = _split3(slope)
    parts = jnp.stack([s1, s2, s3]).astype(F32)
    rows = jnp.concatenate([parts * SEL_BLOCK, parts,
                            jnp.zeros((AUG_POS_ROWS - 6, N_HEADS), F32)])
    slt = jnp.repeat(rows.reshape(AUG_POS_ROWS, N_KV_GROUPS, HEADS_PER_GROUP).transpose(1, 0, 2),
                     TQ, axis=2).astype(BF16)
    lane_i = jnp.arange(HQ) % TQ
    nb = jnp.arange(N_CMP_PAD)
    cmpd = jnp.where(nb[:, None] < N_CMP, nb[:, None] * CMP_STRIDE + (CMP_BLOCK - 1) - lane_i[None, :],
                     2 * SEQ).astype(jnp.int32)
    kc_i = jnp.arange(TKD)
    tri = jnp.where(kc_i[:, None] <= lane_i[None, :], 0.0, NEG_INF).astype(F32)
    upper = jnp.where(kc_i[:, None] > lane_i[None, :], 0.0, NEG_INF).astype(F32)

    nq = SEQ // TQ
    n_vt = SEQ // LANES
    const = lambda shape: pl.BlockSpec(shape, lambda b, g, i: (0,) * len(shape))
    return pl.pallas_call(
        _attn_kernel,
        grid=(BATCH, N_KV_GROUPS, nq),
        in_specs=[
            pl.BlockSpec((1, hw, TQ), lambda b, g, i: (b, g, i)),
            pl.BlockSpec((1, SEQ, LANES), lambda b, g, i: (b, 0, g)),
            pl.BlockSpec((1, n_vt, LANES, LANES), lambda b, g, i: (b, 0, g, 0)),
            pl.BlockSpec((1, 1, N_CMP_PAD, AUG), lambda b, g, i: (b, g, 0, 0)),
            pl.BlockSpec((1, 1, HEAD_DIM, N_CMP_PAD), lambda b, g, i: (b, g, 0, 0)),
            pl.BlockSpec((1, GATE_ROWS, TQ), lambda b, g, i: (b, g, i)),
            const((LANES, AUG)),
            const((LANES, AUG)),
            const((SEQ, AUG)),
            const((SEQ, AUG)),
            const((N_SEL_BLOCKS, N_CMP_PAD)),
            pl.BlockSpec((1, AUG_POS_ROWS, HQ), lambda b, g, i: (g, 0, 0)),
            const((N_CMP_PAD, HQ)),
            const((TKD, HQ)),
            const((TKD, HQ)),
        ],
        out_specs=pl.BlockSpec((1, TQ, hw), lambda b, g, i: (b, i, g)),
        out_shape=jax.ShapeDtypeStruct((BATCH, SEQ, D_MODEL), BF16),
        scratch_shapes=[
            pltpu.VMEM((SEQ, AUG), BF16),
            pltpu.VMEM((SEQ, AUG), BF16),
            pltpu.VMEM((n_vt, V_ROWS, LANES), BF16),
            pltpu.VMEM((n_vt, V_ROWS, LANES), BF16),
            pltpu.VMEM((AUG, HQ), BF16),
            pltpu.VMEM((1, HQ), F32),
            pltpu.VMEM((V_ROWS, HQ), F32),
        ],
        compiler_params=_cparams(3),
        name="nsa_attention",
    )(qt, kk, vt, kc, vct, gt, rs, rw, consts, constw, ovlt, slt, cmpd, tri, upper)


def _out_kernel(o_ref, x_ref, mod_ref, w_ref, lng_ref, lnb_ref, y_ref):
    y = jnp.dot(o_ref[0], w_ref[...], preferred_element_type=F32)
    gate = mod_ref[0][2:3]
    y_ref[0] = _deepnorm_ln(x_ref[0], gate, y, lng_ref[...], lnb_ref[...])


def _out_proj(o, x, mod, w_out, ln_g, ln_b):
    row = lambda b, t: (b, t, 0)
    c2 = lambda b, t: (0, 0)
    return pl.pallas_call(
        _out_kernel,
        grid=(BATCH, SEQ // TM),
        in_specs=[
            pl.BlockSpec((1, TM, D_MODEL), row),
            pl.BlockSpec((1, TM, D_MODEL), row),
            pl.BlockSpec((1, 6, D_MODEL), lambda b, t: (b, 0, 0)),
            pl.BlockSpec((D_MODEL, D_MODEL), c2),
            pl.BlockSpec((1, D_MODEL), c2),
            pl.BlockSpec((1, D_MODEL), c2),
        ],
        out_specs=pl.BlockSpec((1, TM, D_MODEL), row),
        out_shape=jax.ShapeDtypeStruct((BATCH, SEQ, D_MODEL), F32),
        compiler_params=_cparams(2),
        name="nsa_out_proj",
    )(o, x, mod, w_out.astype(BF16), ln_g.reshape(1, D_MODEL), ln_b.reshape(1, D_MODEL))


def _nsa_layer(x, mod, w_in, pos_k, pos_v, k_w1, k_w2, v_w1, v_w2, w_out, ln_g, ln_b):
    qt, kk, vt, gt, kvc = _nsa_proj(x, mod, w_in)
    kc, vct = _compress(kvc, pos_k, pos_v, k_w1, k_w2, v_w1, v_w2)
    o = _attention(qt, kk, vt, kc, vct, gt)
    return _out_proj(o, x, mod, w_out, ln_g, ln_b)


def kernel(x, c, ada_w, ada_b, ln1_g, ln1_b, ln2_g, ln2_b, ffn_w_in, ffn_conv, ffn_w_out, conv_w_in, conv_w,
           conv_w_out, pool_w_in, pool_w_grp, pool_scale, pool_w_out, nsa_w_in, nsa_cmp_pos_k, nsa_cmp_pos_v,
           nsa_cmp_k_w1, nsa_cmp_k_w2, nsa_cmp_v_w1, nsa_cmp_v_w2, nsa_w_out):
    assert x.shape == (BATCH, SEQ, D_MODEL) and x.dtype == F32
    mods = _modulation(c, ada_w, ada_b).reshape(DEPTH, BATCH, 6, D_MODEL)
    for i in range(DEPTH):
        mod = mods[i]
        m, j = i % N_MIXERS, i // N_MIXERS
        if m == 0:
            x = _gated_layer("conv", x, mod, 0, conv_w_in[j], conv_w[j], conv_w_out[j], ln1_g[i], ln1_b[i])
        elif m == 1:
            x = _pool_layer(x, mod, pool_w_in[j], pool_w_grp[j], pool_scale[j], pool_w_out[j], ln1_g[i], ln1_b[i])
        else:
            x = _nsa_layer(x, mod, nsa_w_in[j], nsa_cmp_pos_k[j], nsa_cmp_pos_v[j], nsa_cmp_k_w1[j],
                           nsa_cmp_k_w2[j], nsa_cmp_v_w1[j], nsa_cmp_v_w2[j], nsa_w_out[j], ln1_g[i], ln1_b[i])
        x = _gated_layer("ffn", x, mod, 3, ffn_w_in[i], ffn_conv[i], ffn_w_out[i], ln2_g[i], ln2_b[i])
    return x
```

```python
import functools
import math

import jax
import jax.numpy as jnp
from jax import lax
from jax.experimental import pallas as pl
from jax.experimental.pallas import tpu as pltpu

F32 = jnp.float32
BF16 = jnp.bfloat16

D_MODEL = 1024
BATCH = 16
SEQ = 4096
DEPTH = 4
N_MIXERS = 3
CONV_WIDTH = 3
D_FF = 2816
POOL_WINDOWS = (2, 4, 8, 16)
N_POOL_GROUPS = 4
POOL_GROUP = D_MODEL // N_POOL_GROUPS
N_HEADS = 16
HEAD_DIM = 64
N_KV_GROUPS = 4
HEADS_PER_GROUP = 4
KV_WIDTH = N_KV_GROUPS * HEAD_DIM
CMP_BLOCK = 32
CMP_STRIDE = 16
CMP_HIDDEN = 256
SEL_BLOCK = 64
N_SELECT = 16
WINDOW = 512
N_BRANCH = 3
ALPHA = (2.0 * DEPTH) ** 0.25
LN_EPS = 1e-5
NEG_INF = -1e30
FORCE_SCORE = 1e4
LOG2E = math.log2(math.e)

N_CMP = (SEQ - CMP_BLOCK) // CMP_STRIDE + 1
N_CMP_PAD = 256
N_SEL_BLOCKS = SEQ // SEL_BLOCK
LANES = 128
SUBLANES = 8

TM = 512
FC = 256
CONV_HALO = 8
POOL_HALO = 16
TQ = 128
N_CHAIN = 2
TQ2 = N_CHAIN * TQ
HQ = HEADS_PER_GROUP * TQ
TKS = 512
TKW = WINDOW + TQ
VMEM_LIMIT = 56 * 1024 * 1024

AUG = 256
AUG_SEL0 = HEAD_DIM
AUG_POS0 = 2 * HEAD_DIM
AUG_POS_ROWS = 16
AUG_PAD = AUG_POS0 + AUG_POS_ROWS
AUG_FIXED_ROWS = 2 * AUG_POS_ROWS
V_ROWS = 80
GATE_ROWS = 16


def _cparams(n_axes):
    return pltpu.CompilerParams(dimension_semantics=("arbitrary",) * n_axes,
                                vmem_limit_bytes=VMEM_LIMIT)


def _log2(n):
    assert n & (n - 1) == 0
    return n.bit_length() - 1


def _gelu(x):
    c = math.sqrt(2.0 / math.pi)
    return x * (0.5 * (1.0 + jnp.tanh(c * (x + 0.044715 * (x * x * x)))))


def _deepnorm_ln(x, gate, y, ln_g, ln_b):
    r = ALPHA * x + (1.0 + gate) * y
    mu = jnp.mean(r, axis=-1, keepdims=True)
    d = r - mu
    var = jnp.mean(d * d, axis=-1, keepdims=True)
    return d * lax.rsqrt(var + LN_EPS) * ln_g + ln_b


def _split3(a):
    a1 = a.astype(BF16)
    r1 = a - a1.astype(F32)
    a2 = r1.astype(BF16)
    a3 = (r1 - a2.astype(F32)).astype(BF16)
    return a1, a2, a3


def _ada_kernel(c_ref, w_ref, b_ref, o_ref):
    c = c_ref[...]
    cond = c * jax.nn.sigmoid(c)
    c1, c2, c3 = _split3(cond)
    w1, w2, w3 = _split3(w_ref[0])
    acc = jnp.dot(c1, w1, preferred_element_type=F32)
    acc += jnp.dot(c1, w2, preferred_element_type=F32)
    acc += jnp.dot(c2, w1, preferred_element_type=F32)
    acc += jnp.dot(c1, w3, preferred_element_type=F32)
    acc += jnp.dot(c2, w2, preferred_element_type=F32)
    acc += jnp.dot(c3, w1, preferred_element_type=F32)
    o_ref[0] = acc + b_ref[0]


def _modulation(c, ada_w, ada_b):
    nc = 1536
    n_col = 6 * D_MODEL // nc
    return pl.pallas_call(
        _ada_kernel,
        grid=(DEPTH, n_col),
        in_specs=[
            pl.BlockSpec((BATCH, D_MODEL), lambda i, j: (0, 0)),
            pl.BlockSpec((1, D_MODEL, nc), lambda i, j: (i, 0, j)),
            pl.BlockSpec((1, 1, nc), lambda i, j: (i, 0, j)),
        ],
        out_specs=pl.BlockSpec((1, BATCH, nc), lambda i, j: (i, 0, j)),
        out_shape=jax.ShapeDtypeStruct((DEPTH, BATCH, 6 * D_MODEL), F32),
        compiler_params=_cparams(2),
        name="modulation",
    )(c, ada_w, ada_b.reshape(DEPTH, 1, 6 * D_MODEL))


def _gated_kernel(mode, n_chunks, mrow, x_ref, mod_ref, win_ref, wconv_ref, wout_ref, lng_ref, lnb_ref,
                  o_ref, h_ref, carry_ref, cbuf_ref, acc_ref):
    t = pl.program_id(1)

    @pl.when(t == 0)
    def _():
        carry_ref[...] = jnp.zeros_like(carry_ref)

    xt = x_ref[0]
    m = mod_ref[0]
    sh, sc, gate = m[mrow:mrow + 1], m[mrow + 1:mrow + 2], m[mrow + 2:mrow + 3]
    h_ref[...] = (xt * (1.0 + sc) + sh).astype(BF16)
    acc_ref[...] = jnp.zeros_like(acc_ref)

    def in_proj(j):
        return jnp.dot(h_ref[...], win_ref[j], preferred_element_type=F32)

    p_next = in_proj(0)
    for j in range(n_chunks):
        p = p_next
        if j + 1 < n_chunks:
            p_next = in_proj(j + 1)
        if mode == "conv":
            gate_in = p[:, :FC]
            cin = p[:, FC:2 * FC] * p[:, 2 * FC:]
        else:
            cin = p[:, :FC]
            gate_in = p[:, FC:]
        cbuf_ref[0:CONV_HALO, :] = carry_ref[j]
        cbuf_ref[CONV_HALO:, :] = cin
        carry_ref[j] = cin[TM - CONV_HALO:, :]
        w = wconv_ref[j]
        y = (w[0:1] * cbuf_ref[CONV_HALO - 2:CONV_HALO - 2 + TM, :]
             + w[1:2] * cbuf_ref[CONV_HALO - 1:CONV_HALO - 1 + TM, :]
             + w[2:3] * cin)
        if mode == "conv":
            z = gate_in * y
        else:
            z = _gelu(y) * gate_in
        acc_ref[...] += jnp.dot(z.astype(BF16), wout_ref[j], preferred_element_type=F32)

    o_ref[0] = _deepnorm_ln(xt, gate, acc_ref[...], lng_ref[...], lnb_ref[...])


def _gated_layer(mode, x, mod, mrow, w_in, w_conv, w_out, ln_g, ln_b):
    feat = w_out.shape[0]
    n_parts = w_in.shape[1] // feat
    n_chunks = feat // FC
    win_r = (w_in.astype(BF16).reshape(D_MODEL, n_parts, n_chunks, FC)
             .transpose(2, 0, 1, 3).reshape(n_chunks, D_MODEL, n_parts * FC))
    wout_r = w_out.astype(BF16).reshape(n_chunks, FC, D_MODEL)
    wconv_r = w_conv.reshape(CONV_WIDTH, n_chunks, FC).transpose(1, 0, 2)
    const3 = lambda b, t: (0, 0, 0)
    return pl.pallas_call(
        functools.partial(_gated_kernel, mode, n_chunks, mrow),
        grid=(BATCH, SEQ // TM),
        in_specs=[
            pl.BlockSpec((1, TM, D_MODEL), lambda b, t: (b, t, 0)),
            pl.BlockSpec((1, 6, D_MODEL), lambda b, t: (b, 0, 0)),
            pl.BlockSpec((n_chunks, D_MODEL, n_parts * FC), const3),
            pl.BlockSpec((n_chunks, CONV_WIDTH, FC), const3),
            pl.BlockSpec((n_chunks, FC, D_MODEL), const3),
            pl.BlockSpec((1, D_MODEL), lambda b, t: (0, 0)),
            pl.BlockSpec((1, D_MODEL), lambda b, t: (0, 0)),
        ],
        out_specs=pl.BlockSpec((1, TM, D_MODEL), lambda b, t: (b, t, 0)),
        out_shape=jax.ShapeDtypeStruct((BATCH, SEQ, D_MODEL), F32),
        scratch_shapes=[
            pltpu.VMEM((TM, D_MODEL), BF16),
            pltpu.VMEM((n_chunks, CONV_HALO, FC), F32),
            pltpu.VMEM((CONV_HALO + TM, FC), F32),
            pltpu.VMEM((TM, D_MODEL), F32),
        ],
        compiler_params=_cparams(2),
        name="gated_" + mode,
    )(x, mod, win_r, wconv_r, wout_r, ln_g.reshape(1, D_MODEL), ln_b.reshape(1, D_MODEL))


def _pool_kernel(x_ref, mod_ref, win_ref, wgrp_ref, scale_ref, wout_ref, lng_ref, lnb_ref,
                 o_ref, carry_ref, ubuf_ref, z_ref):
    t = pl.program_id(1)

    @pl.when(t == 0)
    def _():
        carry_ref[...] = jnp.zeros_like(carry_ref)

    xt = x_ref[0]
    m = mod_ref[0]
    sh, sc, gate = m[0:1], m[1:2], m[2:3]
    h = (xt * (1.0 + sc) + sh).astype(BF16)
    u = jnp.dot(h, win_ref[...], preferred_element_type=F32)
    ubuf_ref[0:POOL_HALO, :] = carry_ref[...]
    ubuf_ref[POOL_HALO:, :] = u
    carry_ref[...] = u[TM - POOL_HALO:, :]
    pos = t * TM + lax.broadcasted_iota(jnp.int32, (TM, 1), 0)
    for g, w in enumerate(POOL_WINDOWS):
        lo, hi = g * POOL_GROUP, (g + 1) * POOL_GROUP
        e = ubuf_ref[:, lo:hi]
        s = e
        k = 1
        while k < w:
            s = s + pltpu.roll(s, k, axis=0)
            k *= 2
        cnt = jnp.minimum(pos + 1, w).astype(F32)
        pooled = s[POOL_HALO:, :] / cnt - e[POOL_HALO:, :]
        zg = jnp.dot(pooled.astype(BF16), wgrp_ref[g], preferred_element_type=F32)
        z_ref[:, lo:hi] = (zg * scale_ref[:, lo:hi]).astype(BF16)
    y = jnp.dot(z_ref[...], wout_ref[...], preferred_element_type=F32)
    o_ref[0] = _deepnorm_ln(xt, gate, y, lng_ref[...], lnb_ref[...])


def _pool_layer(x, mod, w_in, w_grp, scale, w_out, ln_g, ln_b):
    c2 = lambda b, t: (0, 0)
    return pl.pallas_call(
        _pool_kernel,
        grid=(BATCH, SEQ // TM),
        in_specs=[
            pl.BlockSpec((1, TM, D_MODEL), lambda b, t: (b, t, 0)),
            pl.BlockSpec((1, 6, D_MODEL), lambda b, t: (b, 0, 0)),
            pl.BlockSpec((D_MODEL, D_MODEL), c2),
            pl.BlockSpec((N_POOL_GROUPS, POOL_GROUP, POOL_GROUP), lambda b, t: (0, 0, 0)),
            pl.BlockSpec((1, D_MODEL), c2),
            pl.BlockSpec((D_MODEL, D_MODEL), c2),
            pl.BlockSpec((1, D_MODEL), c2),
            pl.BlockSpec((1, D_MODEL), c2),
        ],
        out_specs=pl.BlockSpec((1, TM, D_MODEL), lambda b, t: (b, t, 0)),
        out_shape=jax.ShapeDtypeStruct((BATCH, SEQ, D_MODEL), F32),
        scratch_shapes=[
            pltpu.VMEM((POOL_HALO, D_MODEL), F32),
            pltpu.VMEM((POOL_HALO + TM, D_MODEL), F32),
            pltpu.VMEM((TM, D_MODEL), BF16),
        ],
        compiler_params=_cparams(2),
        name="pool_mixer",
    )(x, mod, w_in.astype(BF16), w_grp.astype(BF16), scale.reshape(1, D_MODEL), w_out.astype(BF16),
      ln_g.reshape(1, D_MODEL), ln_b.reshape(1, D_MODEL))


NSA_T_ROWS = D_MODEL + 2 * KV_WIDTH + N_KV_GROUPS * GATE_ROWS
Q_SCALE = HEAD_DIM ** -0.5 * LOG2E


def _nsa_proj_kernel(x_ref, mod_ref, wn_ref, wt_ref, qt_ref, kk_ref, vt_ref, gt_ref, kvc_ref):
    xt = x_ref[0]
    m = mod_ref[0]
    sh, sc = m[0:1], m[1:2]
    h = (xt * (1.0 + sc) + sh).astype(BF16)
    rn = jnp.dot(h, wn_ref[...], preferred_element_type=F32)
    kk_ref[0] = rn[:, :2 * KV_WIDTH].astype(BF16)
    kvc_ref[0] = rn[:, 2 * KV_WIDTH:]
    rt = lax.dot_general(wt_ref[...], h, (((1,), (1,)), ((), ())), preferred_element_type=F32)
    qt_ref[0] = (rt[0:D_MODEL] * Q_SCALE).astype(BF16)
    v_t = rt[D_MODEL:D_MODEL + 2 * KV_WIDTH].astype(BF16)
    for j in range(TM // LANES):
        vt_ref[0, j] = v_t[:, j * LANES:(j + 1) * LANES]
    gt_ref[0] = jax.nn.sigmoid(rt[D_MODEL + 2 * KV_WIDTH:])


def _nsa_proj(x, mod, w_in):
    kc0 = D_MODEL
    vc0, ks0, vs0, kw0, vw0, gl0 = (kc0 + i * KV_WIDTH for i in range(1, 7))
    grp = lambda c0, g: jnp.arange(c0 + g * HEAD_DIM, c0 + (g + 1) * HEAD_DIM)
    cols_n = jnp.concatenate([jnp.concatenate([grp(ks0, g), grp(kw0, g)]) for g in range(N_KV_GROUPS)]
                             + [jnp.arange(kc0, ks0)])
    wn = jnp.take(w_in, cols_n, axis=1).astype(BF16)
    cols_v = jnp.concatenate([jnp.concatenate([grp(vs0, g), grp(vw0, g)]) for g in range(N_KV_GROUPS)])
    n_gate = HEADS_PER_GROUP * N_BRANCH
    w_gate = w_in[:, gl0:gl0 + N_KV_GROUPS * n_gate].reshape(D_MODEL, N_KV_GROUPS, n_gate)
    w_gate = jnp.pad(w_gate, ((0, 0), (0, 0), (0, GATE_ROWS - n_gate))).reshape(D_MODEL, N_KV_GROUPS * GATE_ROWS)
    wt = jnp.concatenate([w_in[:, :D_MODEL], jnp.take(w_in, cols_v, axis=1), w_gate], axis=1).T.astype(BF16)
    row = lambda b, t: (b, t, 0)
    colt = lambda b, t: (b, 0, t)
    return pl.pallas_call(
        _nsa_proj_kernel,
        grid=(BATCH, SEQ // TM),
        in_specs=[
            pl.BlockSpec((1, TM, D_MODEL), row),
            pl.BlockSpec((1, 6, D_MODEL), lambda b, t: (b, 0, 0)),
            pl.BlockSpec((D_MODEL, 4 * KV_WIDTH), lambda b, t: (0, 0)),
            pl.BlockSpec((NSA_T_ROWS, D_MODEL), lambda b, t: (0, 0)),
        ],
        out_specs=[
            pl.BlockSpec((1, D_MODEL, TM), colt),
            pl.BlockSpec((1, TM, 2 * KV_WIDTH), row),
            pl.BlockSpec((1, TM // LANES, 2 * KV_WIDTH, LANES), lambda b, t: (b, t, 0, 0)),
            pl.BlockSpec((1, N_KV_GROUPS * GATE_ROWS, TM), colt),
            pl.BlockSpec((1, TM, 2 * KV_WIDTH), row),
        ],
        out_shape=[
            jax.ShapeDtypeStruct((BATCH, D_MODEL, SEQ), BF16),
            jax.ShapeDtypeStruct((BATCH, SEQ, 2 * KV_WIDTH), BF16),
            jax.ShapeDtypeStruct((BATCH, SEQ // LANES, 2 * KV_WIDTH, LANES), BF16),
            jax.ShapeDtypeStruct((BATCH, N_KV_GROUPS * GATE_ROWS, SEQ), F32),
            jax.ShapeDtypeStruct((BATCH, SEQ, 2 * KV_WIDTH), F32),
        ],
        compiler_params=_cparams(2),
        name="nsa_proj",
    )(x, mod, wn, wt)


CHUNK_FEAT = CMP_STRIDE * HEAD_DIM


def _cmp_hidden(x_ref, posa_ref, posb_ref, w1a_ref, w1b_ref):
    x = x_ref[0, 0]
    xa = (x + posa_ref[0]).astype(BF16)
    xb = (x + posb_ref[0]).astype(BF16)
    a = jnp.dot(xa, w1a_ref[0], preferred_element_type=F32)
    b = jnp.dot(xb, w1b_ref[0], preferred_element_type=F32)
    return _gelu(a + pltpu.roll(b, N_CMP_PAD - 1, axis=0)).astype(BF16)


def _cmp_kernel(xk_ref, xv_ref, posa_ref, posb_ref, w1a_ref, w1b_ref, w2k_ref, w2vt_ref, posc_ref, kc_ref, vct_ref):
    hk = _cmp_hidden(xk_ref, posa_ref.at[0:1], posb_ref.at[0:1], w1a_ref.at[0:1], w1b_ref.at[0:1])
    kc_ref[0, 0] = (jnp.dot(hk, w2k_ref[...], preferred_element_type=F32) + posc_ref[...]).astype(BF16)
    hv = _cmp_hidden(xv_ref, posa_ref.at[1:2], posb_ref.at[1:2], w1a_ref.at[1:2], w1b_ref.at[1:2])
    vct = lax.dot_general(w2vt_ref[...], hv, (((1,), (1,)), ((), ())), preferred_element_type=F32)
    n = lax.broadcasted_iota(jnp.int32, vct.shape, 1)
    vct_ref[0, 0] = jnp.where(n < N_CMP, vct, 0.0).astype(BF16)


def _pos_features(pos_hi, pos_lo, n):
    feat = jnp.stack([pos_hi] * 3 + [pos_lo] * 3, axis=1).astype(F32)
    return jnp.pad(feat, ((0, 0), (AUG_POS0, AUG - AUG_POS0 - 6)))


def _compress(kvc, pos_k, pos_v, k_w1, k_w2, v_w1, v_w2):
    n_chunk = SEQ // CMP_STRIDE
    x = (kvc.reshape(BATCH, n_chunk, CMP_STRIDE, 2 * N_KV_GROUPS, HEAD_DIM)
         .transpose(0, 3, 1, 2, 4).reshape(BATCH, 2 * N_KV_GROUPS, n_chunk, CHUNK_FEAT))
    pos = jnp.stack([pos_k, pos_v])
    posa = pos[:, :CMP_STRIDE].reshape(2, 1, CHUNK_FEAT)
    posb = pos[:, CMP_STRIDE:].reshape(2, 1, CHUNK_FEAT)
    w1 = jnp.stack([k_w1, v_w1]).astype(BF16)
    w1a = w1[:, :CMP_STRIDE].reshape(2, CHUNK_FEAT, CMP_HIDDEN)
    w1b = w1[:, CMP_STRIDE:].reshape(2, CHUNK_FEAT, CMP_HIDDEN)
    w2k = jnp.pad(k_w2, ((0, 0), (0, AUG - HEAD_DIM))).astype(BF16)
    w2vt = v_w2.T.astype(BF16)
    nb = jnp.arange(N_CMP_PAD)
    posc = _pos_features(nb // 4, (nb % 4) * CMP_STRIDE + 0.5 * (CMP_BLOCK - 1), N_CMP_PAD)
    full = lambda shape: pl.BlockSpec(shape, lambda b, g: (0,) * len(shape))
    return pl.pallas_call(
        _cmp_kernel,
        grid=(BATCH, N_KV_GROUPS),
        in_specs=[
            pl.BlockSpec((1, 1, n_chunk, CHUNK_FEAT), lambda b, g: (b, g, 0, 0)),
            pl.BlockSpec((1, 1, n_chunk, CHUNK_FEAT), lambda b, g: (b, N_KV_GROUPS + g, 0, 0)),
            full((2, 1, CHUNK_FEAT)),
            full((2, 1, CHUNK_FEAT)),
            full((2, CHUNK_FEAT, CMP_HIDDEN)),
            full((2, CHUNK_FEAT, CMP_HIDDEN)),
            full((CMP_HIDDEN, AUG)),
            full((HEAD_DIM, CMP_HIDDEN)),
            full((N_CMP_PAD, AUG)),
        ],
        out_specs=[
            pl.BlockSpec((1, 1, N_CMP_PAD, AUG), lambda b, g: (b, g, 0, 0)),
            pl.BlockSpec((1, 1, HEAD_DIM, N_CMP_PAD), lambda b, g: (b, g, 0, 0)),
        ],
        out_shape=[
            jax.ShapeDtypeStruct((BATCH, N_KV_GROUPS, N_CMP_PAD, AUG), BF16),
            jax.ShapeDtypeStruct((BATCH, N_KV_GROUPS, HEAD_DIM, N_CMP_PAD), BF16),
        ],
        compiler_params=_cparams(2),
        name="nsa_compress",
    )(x, x, posa, posb, w1a, w1b, w2k, w2vt, posc)


def _select_top_n(score):
    n_sub = N_SEL_BLOCKS // SUBLANES
    pieces = [score[SUBLANES * v:SUBLANES * (v + 1), :] for v in range(n_sub)]
    ranks = [jnp.zeros((SUBLANES, TQ), F32) for _ in range(n_sub)]
    sub = lax.broadcasted_iota(jnp.int32, (SUBLANES, TQ), 0)
    for j2 in range(N_SEL_BLOCKS):
        v2, u2 = divmod(j2, SUBLANES)
        other = jnp.broadcast_to(score[j2:j2 + 1, :], (SUBLANES, TQ))
        for v in range(n_sub):
            if v < v2:
                beats = other > pieces[v]
            elif v > v2:
                beats = other >= pieces[v]
            else:
                beats = (other > pieces[v]) | ((other == pieces[v]) & (sub > u2))
            ranks[v] = ranks[v] + jnp.where(beats, 1.0, 0.0)
    return jnp.concatenate(ranks, axis=0) < N_SELECT


def _attn_kernel(qt_ref, kk_ref, vt_ref, kc_ref, vct_ref, gt_ref, rs_ref, rw_ref, consts_ref, constw_ref,
                 ovlt_ref, fixed_ref, cmpd_ref, tri_ref, tri2_ref, wmask_ref,
                 o_ref, ksa_ref, kwa_ref, vst_ref, vwt_ref, wq_ref, m_ref, acc_ref):
    qi = pl.program_id(2)
    q0 = qi * TQ2
    n_vt = SEQ // LANES
    n_wpad = WINDOW // LANES

    @pl.when(qi == 0)
    def _():
        rows = 512

        def fill(i, carry):
            r0 = pl.multiple_of(i * rows, rows)
            kk = kk_ref[0, pl.ds(r0, rows), :]
            ksa_ref[pl.ds(r0, rows), :] = (jnp.dot(kk, rs_ref[...], preferred_element_type=F32)
                                           + consts_ref[pl.ds(r0, rows), :].astype(F32)).astype(BF16)
            kwa_ref[pl.ds(WINDOW + r0, rows), :] = (jnp.dot(kk, rw_ref[...], preferred_element_type=F32)
                                                    + constw_ref[pl.ds(r0, rows), :].astype(F32)).astype(BF16)
            return carry

        lax.fori_loop(0, SEQ // rows, fill, 0)
        kwa_ref[0:WINDOW, :] = (lax.broadcasted_iota(jnp.int32, (WINDOW, AUG), 1) == AUG_PAD).astype(BF16)
        ones_pad = (lax.broadcasted_iota(jnp.int32, (n_vt, V_ROWS - HEAD_DIM, LANES), 1) == 0).astype(BF16)
        vst_ref[:, 0:HEAD_DIM, :] = vt_ref[0, :, 0:HEAD_DIM, :]
        vst_ref[:, HEAD_DIM:, :] = ones_pad
        vwt_ref[0:n_wpad] = jnp.zeros((n_wpad, V_ROWS, LANES), BF16)
        vwt_ref[n_wpad:, 0:HEAD_DIM, :] = vt_ref[0, :, HEAD_DIM:, :]
        vwt_ref[n_wpad:, HEAD_DIM:, :] = ones_pad
        wq_ref[AUG_SEL0:AUG_SEL0 + N_SEL_BLOCKS, :] = jnp.zeros((N_SEL_BLOCKS, N_CHAIN * HQ), BF16)
        wq_ref[AUG_POS0:AUG_POS0 + AUG_FIXED_ROWS, :] = fixed_ref[0]
        wq_ref[AUG_POS0 + AUG_FIXED_ROWS:, :] = jnp.zeros((AUG - AUG_POS0 - AUG_FIXED_ROWS, N_CHAIN * HQ), BF16)

    qt = qt_ref[0]
    for c in range(N_CHAIN):
        for r in range(HEADS_PER_GROUP):
            lane0 = c * HQ + r * TQ
            wq_ref[0:HEAD_DIM, lane0:lane0 + TQ] = qt[r * HEAD_DIM:(r + 1) * HEAD_DIM, c * TQ:(c + 1) * TQ]

    def wq_chain(c):
        return wq_ref[:, c * HQ:(c + 1) * HQ]

    lane_q = lax.broadcasted_iota(jnp.int32, (1, HQ), 1) & (TQ - 1)

    def run_jobs(jobs, ahead=2):
        scores = [None] * len(jobs)
        for i in range(min(ahead, len(jobs))):
            scores[i] = jobs[i][0]()
        for i, (_, soft, value) in enumerate(jobs):
            mid = soft(scores[i])
            if i + ahead < len(jobs):
                scores[i + ahead] = jobs[i + ahead][0]()
            value(mid)

    def v_tiles(v_ref, row0, tk):
        vt0 = row0 >> _log2(LANES)
        return jnp.concatenate([v_ref[vt0 + j] for j in range(tk // LANES)], axis=1)

    oc_t = [None] * N_CHAIN
    imp_t = [None] * N_CHAIN

    def cmp_job(c):
        q0c = q0 + c * TQ

        def score():
            return jnp.dot(kc_ref[0, 0], wq_chain(c), preferred_element_type=F32)

        def soft(s):
            s = jnp.where(cmpd_ref[...] <= q0c, s, NEG_INF)
            p = jnp.exp2(s - jnp.max(s, axis=0, keepdims=True))
            has_c = (q0c + lane_q >= CMP_BLOCK - 1).astype(F32)
            p = p * (has_c / jnp.sum(p, axis=0, keepdims=True))
            psum = p[:, 0:TQ] + p[:, TQ:2 * TQ] + p[:, 2 * TQ:3 * TQ] + p[:, 3 * TQ:4 * TQ]
            return p.astype(BF16), _split3(psum)

        def value(mid):
            pb, (p1, p2, p3) = mid
            oc_t[c] = jnp.dot(vct_ref[0, 0], pb, preferred_element_type=F32)
            ovlt = ovlt_ref[...]
            imp_t[c] = (jnp.dot(ovlt, p1, preferred_element_type=F32)
                        + jnp.dot(ovlt, p2, preferred_element_type=F32)
                        + jnp.dot(ovlt, p3, preferred_element_type=F32))

        return score, soft, value

    ow_t = [None] * N_CHAIN

    def win_job(c):
        row0 = pl.multiple_of(q0 + c * TQ, LANES)

        def score():
            return jnp.dot(kwa_ref[pl.ds(row0, TKW), :], wq_chain(c), preferred_element_type=F32)

        def soft(sc):
            sc = sc + wmask_ref[...]
            return jnp.exp2(sc - jnp.max(sc, axis=0, keepdims=True)).astype(BF16)

        def value(pt):
            acc = jnp.dot(v_tiles(vwt_ref, row0, TKW), pt, preferred_element_type=F32)
            ow_t[c] = acc[0:HEAD_DIM] / acc[HEAD_DIM:HEAD_DIM + 1]

        return score, soft, value

    run_jobs([cmp_job(c) for c in range(N_CHAIN)] + [win_job(c) for c in range(N_CHAIN)])

    sel_bias = []
    for c in range(N_CHAIN):
        q0c = q0 + c * TQ
        jb = lax.broadcasted_iota(jnp.int32, (N_SEL_BLOCKS, TQ), 0)
        bt = (q0c + lax.broadcasted_iota(jnp.int32, (N_SEL_BLOCKS, TQ), 1)) >> _log2(SEL_BLOCK)
        forced = (jb == 0) | (jb == bt) | (jb == bt - 1)
        score = jnp.where(forced, FORCE_SCORE, jnp.where(jb <= bt, imp_t[c], -FORCE_SCORE))
        sel_bias.append(jnp.where(_select_top_n(score), 0.0, NEG_INF).astype(BF16))
    for c in range(N_CHAIN):
        for r in range(HEADS_PER_GROUP):
            lane0 = c * HQ + r * TQ
            wq_ref[AUG_SEL0:AUG_SEL0 + N_SEL_BLOCKS, lane0:lane0 + TQ] = sel_bias[c]

    m_ref[...] = jnp.full_like(m_ref, NEG_INF)
    acc_ref[...] = jnp.zeros_like(acc_ref)

    def sel_job(c, row0, tk, mask_bias):
        row0 = pl.multiple_of(row0, LANES)

        def score():
            return jnp.dot(ksa_ref[pl.ds(row0, tk), :], wq_chain(c), preferred_element_type=F32)

        def soft(sc):
            if mask_bias is not None:
                sc = sc + mask_bias[...]
            m_old = m_ref[c]
            m_new = jnp.maximum(m_old, jnp.max(sc, axis=0, keepdims=True))
            m_ref[c] = m_new
            return jnp.exp2(sc - m_new).astype(BF16), jnp.exp2(m_old - m_new)

        def value(mid):
            pt, alpha = mid
            acc_ref[c] = alpha * acc_ref[c] + jnp.dot(v_tiles(vst_ref, row0, tk), pt, preferred_element_type=F32)

        return score, soft, value

    n_full = q0 >> _log2(TKS)

    def sel_pair(kt2, carry):
        run_jobs([sel_job(c, (2 * kt2 + t) * TKS, TKS, None) for t in range(2) for c in range(N_CHAIN)])
        return carry

    lax.fori_loop(0, n_full >> 1, sel_pair, 0)

    def end_jobs(has_odd, has_half):
        jobs = []
        if has_odd:
            jobs += [sel_job(c, (n_full - 1) * TKS, TKS, None) for c in range(N_CHAIN)]
        if has_half:
            jobs += [sel_job(c, q0 - TQ2, TQ2, None) for c in range(N_CHAIN)]
        return jobs + [sel_job(0, q0, TQ, tri_ref), sel_job(1, q0, TQ2, tri2_ref)]

    variant = (q0 >> _log2(TQ2)) & 3
    for v in range(4):
        @pl.when(variant == v)
        def _():
            run_jobs(end_jobs(has_odd=bool(v & 2), has_half=bool(v & 1)))

    os_t = []
    for c in range(N_CHAIN):
        acc = acc_ref[c]
        os_t.append(acc[0:HEAD_DIM] / acc[HEAD_DIM:HEAD_DIM + 1])

    gt = gt_ref[0]
    cols = []
    for c in range(N_CHAIN):
        blocks = []
        for r in range(HEADS_PER_GROUP):
            sl = slice(r * TQ, (r + 1) * TQ)
            g3 = gt[3 * r:3 * r + 3, c * TQ:(c + 1) * TQ]
            blocks.append(g3[0:1] * oc_t[c][:, sl] + g3[1:2] * os_t[c][:, sl] + g3[2:3] * ow_t[c][:, sl])
        cols.append(jnp.concatenate(blocks, axis=0))
    o_ref[0] = jnp.concatenate(cols, axis=1).T.astype(BF16)


def _attention(qt, kk, vt, kc, vct, gt):
    hw = HEADS_PER_GROUP * HEAD_DIM
    src = jnp.arange(LANES)
    col = jnp.arange(AUG)
    rs = (src[:, None] == col[None, :]).astype(BF16) * (src[:, None] < HEAD_DIM)
    rw = (src[:, None] - HEAD_DIM == col[None, :]).astype(BF16) * (src[:, None] >= HEAD_DIM)
    key = jnp.arange(SEQ)
    pos = _pos_features(key // SEL_BLOCK, key % SEL_BLOCK, SEQ)
    onehot = (col[None, :] - AUG_SEL0 == (key // SEL_BLOCK)[:, None]).astype(F32)
    consts = (pos + onehot).astype(BF16)
    constw = pos.astype(BF16)
    cmp_start = jnp.arange(N_CMP_PAD) * CMP_STRIDE
    sel_start = jnp.arange(N_SEL_BLOCKS) * SEL_BLOCK
    ovlt = ((cmp_start[None, :] <= sel_start[:, None] + SEL_BLOCK - 1)
            & (cmp_start[None, :] + CMP_BLOCK - 1 >= sel_start[:, None])
            & (jnp.arange(N_CMP_PAD)[None, :] < N_CMP)).astype(BF16)
    hh = jnp.arange(1, N_HEADS + 1, dtype=F32)
    slope = jnp.exp2(-8.0 * hh / N_HEADS) * LOG2E
    s1, s2, s3 = _split3(slope)
    parts = jnp.stack([s1, s2, s3]).astype(F32)
    rows = jnp.concatenate([parts * SEL_BLOCK, parts,
                            jnp.zeros((AUG_POS_ROWS - 6, N_HEADS), F32)])
    slt = jnp.repeat(rows.reshape(AUG_POS_ROWS, N_KV_GROUPS, HEADS_PER_GROUP).transpose(1, 0, 2),
                     TQ, axis=2)
    pad_rows = jnp.zeros((N_KV_GROUPS, AUG_POS_ROWS, HQ), F32).at[:, 0, :].set(NEG_INF)
    fixed = jnp.tile(jnp.concatenate([slt, pad_rows], axis=1), (1, 1, N_CHAIN)).astype(BF16)
    lane_i = jnp.arange(HQ) % TQ
    nb = jnp.arange(N_CMP_PAD)
    cmpd = jnp.where(nb[:, None] < N_CMP, nb[:, None] * CMP_STRIDE + (CMP_BLOCK - 1) - lane_i[None, :],
                     2 * SEQ).astype(jnp.int32)

    def key_mask(n_keys, lo, hi):
        d = jnp.arange(n_keys)[:, None] - lane_i[None, :]
        return jnp.where((d >= lo) & (d <= hi), 0.0, NEG_INF).astype(F32)

    tri = key_mask(TQ, -SEQ, 0)
    tri2 = key_mask(TQ2, -SEQ, TQ)
    wmask = key_mask(TKW, 1, WINDOW)

    nq = SEQ // TQ2
    n_vt = SEQ // LANES
    const = lambda shape: pl.BlockSpec(shape, lambda b, g, i: (0,) * len(shape))
    return pl.pallas_call(
        _attn_kernel,
        grid=(BATCH, N_KV_GROUPS, nq),
        in_specs=[
            pl.BlockSpec((1, hw, TQ2), lambda b, g, i: (b, g, i)),
            pl.BlockSpec((1, SEQ, LANES), lambda b, g, i: (b, 0, g)),
            pl.BlockSpec((1, n_vt, LANES, LANES), lambda b, g, i: (b, 0, g, 0)),
            pl.BlockSpec((1, 1, N_CMP_PAD, AUG), lambda b, g, i: (b, g, 0, 0)),
            pl.BlockSpec((1, 1, HEAD_DIM, N_CMP_PAD), lambda b, g, i: (b, g, 0, 0)),
            pl.BlockSpec((1, GATE_ROWS, TQ2), lambda b, g, i: (b, g, i)),
            const((LANES, AUG)),
            const((LANES, AUG)),
            const((SEQ, AUG)),
            const((SEQ, AUG)),
            const((N_SEL_BLOCKS, N_CMP_PAD)),
            pl.BlockSpec((1, AUG_FIXED_ROWS, N_CHAIN * HQ), lambda b, g, i: (g, 0, 0)),
            const((N_CMP_PAD, HQ)),
            const((TQ, HQ)),
            const((TQ2, HQ)),
            const((TKW, HQ)),
        ],
        out_specs=pl.BlockSpec((1, TQ2, hw), lambda b, g, i: (b, i, g)),
        out_shape=jax.ShapeDtypeStruct((BATCH, SEQ, D_MODEL), BF16),
        scratch_shapes=[
            pltpu.VMEM((SEQ, AUG), BF16),
            pltpu.VMEM((WINDOW + SEQ, AUG), BF16),
            pltpu.VMEM((n_vt, V_ROWS, LANES), BF16),
            pltpu.VMEM((WINDOW // LANES + n_vt, V_ROWS, LANES), BF16),
            pltpu.VMEM((AUG, N_CHAIN * HQ), BF16),
            pltpu.VMEM((N_CHAIN, 1, HQ), F32),
            pltpu.VMEM((N_CHAIN, V_ROWS, HQ), F32),
        ],
        compiler_params=_cparams(3),
        name="nsa_attention",
    )(qt, kk, vt, kc, vct, gt, rs, rw, consts, constw, ovlt, fixed, cmpd, tri, tri2, wmask)


def _out_kernel(o_ref, x_ref, mod_ref, w_ref, lng_ref, lnb_ref, y_ref):
    y = jnp.dot(o_ref[0], w_ref[...], preferred_element_type=F32)
    gate = mod_ref[0][2:3]
    y_ref[0] = _deepnorm_ln(x_ref[0], gate, y, lng_ref[...], lnb_ref[...])


def _out_proj(o, x, mod, w_out, ln_g, ln_b):
    row = lambda b, t: (b, t, 0)
    c2 = lambda b, t: (0, 0)
    return pl.pallas_call(
        _out_kernel,
        grid=(BATCH, SEQ // TM),
        in_specs=[
            pl.BlockSpec((1, TM, D_MODEL), row),
            pl.BlockSpec((1, TM, D_MODEL), row),
            pl.BlockSpec((1, 6, D_MODEL), lambda b, t: (b, 0, 0)),
            pl.BlockSpec((D_MODEL, D_MODEL), c2),
            pl.BlockSpec((1, D_MODEL), c2),
            pl.BlockSpec((1, D_MODEL), c2),
        ],
        out_specs=pl.BlockSpec((1, TM, D_MODEL), row),
        out_shape=jax.ShapeDtypeStruct((BATCH, SEQ, D_MODEL), F32),
        compiler_params=_cparams(2),
        name="nsa_out_proj",
    )(o, x, mod, w_out.astype(BF16), ln_g.reshape(1, D_MODEL), ln_b.reshape(1, D_MODEL))


def _nsa_layer(x, mod, w_in, pos_k, pos_v, k_w1, k_w2, v_w1, v_w2, w_out, ln_g, ln_b):
    qt, kk, vt, gt, kvc = _nsa_proj(x, mod, w_in)
    kc, vct = _compress(kvc, pos_k, pos_v, k_w1, k_w2, v_w1, v_w2)
    o = _attention(qt, kk, vt, kc, vct, gt)
    return _out_proj(o, x, mod, w_out, ln_g, ln_b)


def kernel(x, c, ada_w, ada_b, ln1_g, ln1_b, ln2_g, ln2_b, ffn_w_in, ffn_conv, ffn_w_out, conv_w_in, conv_w,
           conv_w_out, pool_w_in, pool_w_grp, pool_scale, pool_w_out, nsa_w_in, nsa_cmp_pos_k, nsa_cmp_pos_v,
           nsa_cmp_k_w1, nsa_cmp_k_w2, nsa_cmp_v_w1, nsa_cmp_v_w2, nsa_w_out):
    assert x.shape == (BATCH, SEQ, D_MODEL) and x.dtype == F32
    mods = _modulation(c, ada_w, ada_b).reshape(DEPTH, BATCH, 6, D_MODEL)
    for i in range(DEPTH):
        mod = mods[i]
        m, j = i % N_MIXERS, i // N_MIXERS
        if m == 0:
            x = _gated_layer("conv", x, mod, 0, conv_w_in[j], conv_w[j], conv_w_out[j], ln1_g[i], ln1_b[i])
        elif m == 1:
            x = _pool_layer(x, mod, pool_w_in[j], pool_w_grp[j], pool_scale[j], pool_w_out[j], ln1_g[i], ln1_b[i])
        else:
            x = _nsa_layer(x, mod, nsa_w_in[j], nsa_cmp_pos_k[j], nsa_cmp_pos_v[j], nsa_cmp_k_w1[j],
                           nsa_cmp_k_w2[j], nsa_cmp_v_w1[j], nsa_cmp_v_w2[j], nsa_w_out[j], ln1_g[i], ln1_b[i])
        x = _gated_layer("ffn", x, mod, 3, ffn_w_in[i], ffn_conv[i], ffn_w_out[i], ln2_g[i], ln2_b[i])
    return x
```

```python
import functools
import math

import jax
import jax.numpy as jnp
from jax import lax
from jax.experimental import pallas as pl
from jax.experimental.pallas import tpu as pltpu

F32 = jnp.float32
BF16 = jnp.bfloat16

D_MODEL = 1024
BATCH = 16
SEQ = 4096
DEPTH = 4
N_MIXERS = 3
CONV_WIDTH = 3
D_FF = 2816
POOL_WINDOWS = (2, 4, 8, 16)
N_POOL_GROUPS = 4
POOL_GROUP = D_MODEL // N_POOL_GROUPS
N_HEADS = 16
HEAD_DIM = 64
N_KV_GROUPS = 4
HEADS_PER_GROUP = 4
KV_WIDTH = N_KV_GROUPS * HEAD_DIM
CMP_BLOCK = 32
CMP_STRIDE = 16
CMP_HIDDEN = 256
SEL_BLOCK = 64
N_SELECT = 16
WINDOW = 512
N_BRANCH = 3
ALPHA = (2.0 * DEPTH) ** 0.25
LN_EPS = 1e-5
NEG_INF = -1e30
FORCE_SCORE = 1e4
LOG2E = math.log2(math.e)

N_CMP = (SEQ - CMP_BLOCK) // CMP_STRIDE + 1
N_CMP_PAD = 256
N_SEL_BLOCKS = SEQ // SEL_BLOCK
LANES = 128
SUBLANES = 8

TM = 512
FC = 256
CONV_HALO = 8
POOL_HALO = 16
TQ = 128
N_CHAIN = 4
TQ2 = N_CHAIN * TQ
HQ = HEADS_PER_GROUP * TQ
TKS = 512
TKJ = 256
TKW = WINDOW + TQ
VMEM_LIMIT = 56 * 1024 * 1024

AUG = 256
AUG_SEL0 = HEAD_DIM
AUG_POS0 = 2 * HEAD_DIM
AUG_POS_ROWS = 16
AUG_PAD = AUG_POS0 + AUG_POS_ROWS
AUG_FIXED_ROWS = 2 * AUG_POS_ROWS
V_ROWS = 80
GATE_ROWS = 16


def _cparams(n_axes):
    return pltpu.CompilerParams(dimension_semantics=("arbitrary",) * n_axes,
                                vmem_limit_bytes=VMEM_LIMIT)


def _log2(n):
    assert n & (n - 1) == 0
    return n.bit_length() - 1


def _gelu(x):
    c = math.sqrt(2.0 / math.pi)
    return x * (0.5 * (1.0 + jnp.tanh(c * (x + 0.044715 * (x * x * x)))))


def _deepnorm_ln(x, gate, y, ln_g, ln_b):
    r = ALPHA * x + (1.0 + gate) * y
    mu = jnp.mean(r, axis=-1, keepdims=True)
    d = r - mu
    var = jnp.mean(d * d, axis=-1, keepdims=True)
    return d * lax.rsqrt(var + LN_EPS) * ln_g + ln_b


def _split3(a):
    a1 = a.astype(BF16)
    r1 = a - a1.astype(F32)
    a2 = r1.astype(BF16)
    a3 = (r1 - a2.astype(F32)).astype(BF16)
    return a1, a2, a3


def _ada_kernel(c_ref, w_ref, b_ref, o_ref):
    c = c_ref[...]
    cond = c * jax.nn.sigmoid(c)
    c1, c2, c3 = _split3(cond)
    w1, w2, w3 = _split3(w_ref[0])
    acc = jnp.dot(c1, w1, preferred_element_type=F32)
    acc += jnp.dot(c1, w2, preferred_element_type=F32)
    acc += jnp.dot(c2, w1, preferred_element_type=F32)
    acc += jnp.dot(c1, w3, preferred_element_type=F32)
    acc += jnp.dot(c2, w2, preferred_element_type=F32)
    acc += jnp.dot(c3, w1, preferred_element_type=F32)
    o_ref[0] = acc + b_ref[0]


def _modulation(c, ada_w, ada_b):
    nc = 1536
    n_col = 6 * D_MODEL // nc
    return pl.pallas_call(
        _ada_kernel,
        grid=(DEPTH, n_col),
        in_specs=[
            pl.BlockSpec((BATCH, D_MODEL), lambda i, j: (0, 0)),
            pl.BlockSpec((1, D_MODEL, nc), lambda i, j: (i, 0, j)),
            pl.BlockSpec((1, 1, nc), lambda i, j: (i, 0, j)),
        ],
        out_specs=pl.BlockSpec((1, BATCH, nc), lambda i, j: (i, 0, j)),
        out_shape=jax.ShapeDtypeStruct((DEPTH, BATCH, 6 * D_MODEL), F32),
        compiler_params=_cparams(2),
        name="modulation",
    )(c, ada_w, ada_b.reshape(DEPTH, 1, 6 * D_MODEL))


def _gated_kernel(mode, n_chunks, mrow, x_ref, mod_ref, win_ref, wconv_ref, wout_ref, lng_ref, lnb_ref,
                  o_ref, h_ref, carry_ref, cbuf_ref, acc_ref):
    t = pl.program_id(1)

    @pl.when(t == 0)
    def _():
        carry_ref[...] = jnp.zeros_like(carry_ref)

    m = mod_ref[0]
    sh, sc, gate = m[mrow:mrow + 1], m[mrow + 1:mrow + 2], m[mrow + 2:mrow + 3]
    feat = n_chunks * FC
    n_parts = 3 if mode == "conv" else 2

    xt = x_ref[0]
    h_ref[...] = (xt * (1.0 + sc) + sh).astype(BF16)

    def in_proj(j):
        h = h_ref[...]
        return [jnp.dot(h, win_ref[:, p * feat + j * FC:p * feat + (j + 1) * FC], preferred_element_type=F32)
                for p in range(n_parts)]

    p_next = in_proj(0)
    for j in range(n_chunks):
        p = p_next
        if j + 1 < n_chunks:
            p_next = in_proj(j + 1)
        if mode == "conv":
            gate_in = p[0]
            cin = p[1] * p[2]
        else:
            cin, gate_in = p
        cbuf_ref[0:CONV_HALO, :] = carry_ref[j]
        cbuf_ref[CONV_HALO:, :] = cin
        carry_ref[j] = cin[TM - CONV_HALO:, :]
        w = wconv_ref[j]
        y = (w[0:1] * cbuf_ref[CONV_HALO - 2:CONV_HALO - 2 + TM, :]
             + w[1:2] * cbuf_ref[CONV_HALO - 1:CONV_HALO - 1 + TM, :]
             + w[2:3] * cin)
        if mode == "conv":
            z = gate_in * y
        else:
            z = _gelu(y) * gate_in
        out = jnp.dot(z.astype(BF16), wout_ref[j], preferred_element_type=F32)
        if j == 0:
            acc_ref[...] = out
        else:
            acc_ref[...] += out

    o_ref[0] = _deepnorm_ln(xt, gate, acc_ref[...], lng_ref[...], lnb_ref[...])


def _gated_layer(mode, x, mod, mrow, w_in, w_conv, w_out, ln_g, ln_b):
    feat = w_out.shape[0]
    n_parts = w_in.shape[1] // feat
    n_chunks = feat // FC
    assert n_parts == (3 if mode == "conv" else 2)
    wout_r = w_out.astype(BF16).reshape(n_chunks, FC, D_MODEL)
    wconv_r = w_conv.reshape(CONV_WIDTH, n_chunks, FC).transpose(1, 0, 2)
    const3 = lambda b, t: (0, 0, 0)
    return pl.pallas_call(
        functools.partial(_gated_kernel, mode, n_chunks, mrow),
        grid=(BATCH, SEQ // TM),
        in_specs=[
            pl.BlockSpec((1, TM, D_MODEL), lambda b, t: (b, t, 0)),
            pl.BlockSpec((1, 6, D_MODEL), lambda b, t: (b, 0, 0)),
            pl.BlockSpec((D_MODEL, n_parts * feat), lambda b, t: (0, 0)),
            pl.BlockSpec((n_chunks, CONV_WIDTH, FC), const3),
            pl.BlockSpec((n_chunks, FC, D_MODEL), const3),
            pl.BlockSpec((1, D_MODEL), lambda b, t: (0, 0)),
            pl.BlockSpec((1, D_MODEL), lambda b, t: (0, 0)),
        ],
        out_specs=pl.BlockSpec((1, TM, D_MODEL), lambda b, t: (b, t, 0)),
        out_shape=jax.ShapeDtypeStruct((BATCH, SEQ, D_MODEL), F32),
        scratch_shapes=[
            pltpu.VMEM((TM, D_MODEL), BF16),
            pltpu.VMEM((n_chunks, CONV_HALO, FC), F32),
            pltpu.VMEM((CONV_HALO + TM, FC), F32),
            pltpu.VMEM((TM, D_MODEL), F32),
        ],
        compiler_params=_cparams(2),
        name="gated_" + mode,
    )(x, mod, w_in.astype(BF16), wconv_r, wout_r, ln_g.reshape(1, D_MODEL), ln_b.reshape(1, D_MODEL))


def _pool_kernel(x_ref, mod_ref, win_ref, wgrp_ref, scale_ref, wout_ref, lng_ref, lnb_ref,
                 o_ref, carry_ref, ubuf_ref, z_ref):
    t = pl.program_id(1)

    @pl.when(t == 0)
    def _():
        carry_ref[...] = jnp.zeros_like(carry_ref)

    xt = x_ref[0]
    m = mod_ref[0]
    sh, sc, gate = m[0:1], m[1:2], m[2:3]
    h = (xt * (1.0 + sc) + sh).astype(BF16)
    u = jnp.dot(h, win_ref[...], preferred_element_type=F32)
    ubuf_ref[0:POOL_HALO, :] = carry_ref[...]
    ubuf_ref[POOL_HALO:, :] = u
    carry_ref[...] = u[TM - POOL_HALO:, :]
    pos = t * TM + lax.broadcasted_iota(jnp.int32, (TM, 1), 0)
    for g, w in enumerate(POOL_WINDOWS):
        lo, hi = g * POOL_GROUP, (g + 1) * POOL_GROUP
        e = ubuf_ref[:, lo:hi]
        s = e
        k = 1
        while k < w:
            s = s + pltpu.roll(s, k, axis=0)
            k *= 2
        cnt = jnp.minimum(pos + 1, w).astype(F32)
        pooled = s[POOL_HALO:, :] / cnt - e[POOL_HALO:, :]
        zg = jnp.dot(pooled.astype(BF16), wgrp_ref[g], preferred_element_type=F32)
        z_ref[:, lo:hi] = (zg * scale_ref[:, lo:hi]).astype(BF16)
    y = jnp.dot(z_ref[...], wout_ref[...], preferred_element_type=F32)
    o_ref[0] = _deepnorm_ln(xt, gate, y, lng_ref[...], lnb_ref[...])


def _pool_layer(x, mod, w_in, w_grp, scale, w_out, ln_g, ln_b):
    c2 = lambda b, t: (0, 0)
    return pl.pallas_call(
        _pool_kernel,
        grid=(BATCH, SEQ // TM),
        in_specs=[
            pl.BlockSpec((1, TM, D_MODEL), lambda b, t: (b, t, 0)),
            pl.BlockSpec((1, 6, D_MODEL), lambda b, t: (b, 0, 0)),
            pl.BlockSpec((D_MODEL, D_MODEL), c2),
            pl.BlockSpec((N_POOL_GROUPS, POOL_GROUP, POOL_GROUP), lambda b, t: (0, 0, 0)),
            pl.BlockSpec((1, D_MODEL), c2),
            pl.BlockSpec((D_MODEL, D_MODEL), c2),
            pl.BlockSpec((1, D_MODEL), c2),
            pl.BlockSpec((1, D_MODEL), c2),
        ],
        out_specs=pl.BlockSpec((1, TM, D_MODEL), lambda b, t: (b, t, 0)),
        out_shape=jax.ShapeDtypeStruct((BATCH, SEQ, D_MODEL), F32),
        scratch_shapes=[
            pltpu.VMEM((POOL_HALO, D_MODEL), F32),
            pltpu.VMEM((POOL_HALO + TM, D_MODEL), F32),
            pltpu.VMEM((TM, D_MODEL), BF16),
        ],
        compiler_params=_cparams(2),
        name="pool_mixer",
    )(x, mod, w_in.astype(BF16), w_grp.astype(BF16), scale.reshape(1, D_MODEL), w_out.astype(BF16),
      ln_g.reshape(1, D_MODEL), ln_b.reshape(1, D_MODEL))


NSA_T_ROWS = D_MODEL + 2 * KV_WIDTH + N_KV_GROUPS * GATE_ROWS
Q_SCALE = HEAD_DIM ** -0.5 * LOG2E


def _nsa_proj_kernel(x_ref, mod_ref, wn_ref, wt_ref, qt_ref, kk_ref, vt_ref, gt_ref, kvc_ref):
    xt = x_ref[0]
    m = mod_ref[0]
    sh, sc = m[0:1], m[1:2]
    h = (xt * (1.0 + sc) + sh).astype(BF16)
    rn = jnp.dot(h, wn_ref[...], preferred_element_type=F32)
    kk_ref[0] = rn[:, :2 * KV_WIDTH].astype(BF16)
    kvc_ref[0] = rn[:, 2 * KV_WIDTH:]
    rt = lax.dot_general(wt_ref[...], h, (((1,), (1,)), ((), ())), preferred_element_type=F32)
    qt_ref[0] = (rt[0:D_MODEL] * Q_SCALE).astype(BF16)
    v_t = rt[D_MODEL:D_MODEL + 2 * KV_WIDTH].astype(BF16)
    for j in range(TM // LANES):
        vt_ref[0, j] = v_t[:, j * LANES:(j + 1) * LANES]
    gt_ref[0] = jax.nn.sigmoid(rt[D_MODEL + 2 * KV_WIDTH:])


def _nsa_proj(x, mod, w_in):
    kc0 = D_MODEL
    vc0, ks0, vs0, kw0, vw0, gl0 = (kc0 + i * KV_WIDTH for i in range(1, 7))
    grp = lambda c0, g: jnp.arange(c0 + g * HEAD_DIM, c0 + (g + 1) * HEAD_DIM)
    cols_n = jnp.concatenate([jnp.concatenate([grp(ks0, g), grp(kw0, g)]) for g in range(N_KV_GROUPS)]
                             + [jnp.arange(kc0, ks0)])
    wn = jnp.take(w_in, cols_n, axis=1).astype(BF16)
    cols_v = jnp.concatenate([jnp.concatenate([grp(vs0, g), grp(vw0, g)]) for g in range(N_KV_GROUPS)])
    n_gate = HEADS_PER_GROUP * N_BRANCH
    w_gate = w_in[:, gl0:gl0 + N_KV_GROUPS * n_gate].reshape(D_MODEL, N_KV_GROUPS, n_gate)
    w_gate = jnp.pad(w_gate, ((0, 0), (0, 0), (0, GATE_ROWS - n_gate))).reshape(D_MODEL, N_KV_GROUPS * GATE_ROWS)
    wt = jnp.concatenate([w_in[:, :D_MODEL], jnp.take(w_in, cols_v, axis=1), w_gate], axis=1).T.astype(BF16)
    row = lambda b, t: (b, t, 0)
    colt = lambda b, t: (b, 0, t)
    return pl.pallas_call(
        _nsa_proj_kernel,
        grid=(BATCH, SEQ // TM),
        in_specs=[
            pl.BlockSpec((1, TM, D_MODEL), row),
            pl.BlockSpec((1, 6, D_MODEL), lambda b, t: (b, 0, 0)),
            pl.BlockSpec((D_MODEL, 4 * KV_WIDTH), lambda b, t: (0, 0)),
            pl.BlockSpec((NSA_T_ROWS, D_MODEL), lambda b, t: (0, 0)),
        ],
        out_specs=[
            pl.BlockSpec((1, D_MODEL, TM), colt),
            pl.BlockSpec((1, TM, 2 * KV_WIDTH), row),
            pl.BlockSpec((1, TM // LANES, 2 * KV_WIDTH, LANES), lambda b, t: (b, t, 0, 0)),
            pl.BlockSpec((1, N_KV_GROUPS * GATE_ROWS, TM), colt),
            pl.BlockSpec((1, TM, 2 * KV_WIDTH), row),
        ],
        out_shape=[
            jax.ShapeDtypeStruct((BATCH, D_MODEL, SEQ), BF16),
            jax.ShapeDtypeStruct((BATCH, SEQ, 2 * KV_WIDTH), BF16),
            jax.ShapeDtypeStruct((BATCH, SEQ // LANES, 2 * KV_WIDTH, LANES), BF16),
            jax.ShapeDtypeStruct((BATCH, N_KV_GROUPS * GATE_ROWS, SEQ), F32),
            jax.ShapeDtypeStruct((BATCH, SEQ, 2 * KV_WIDTH), F32),
        ],
        compiler_params=_cparams(2),
        name="nsa_proj",
    )(x, mod, wn, wt)


CHUNK_FEAT = CMP_STRIDE * HEAD_DIM


def _cmp_hidden(x_ref, posa_ref, posb_ref, w1a_ref, w1b_ref):
    x = x_ref[0, 0]
    xa = (x + posa_ref[0]).astype(BF16)
    xb = (x + posb_ref[0]).astype(BF16)
    a = jnp.dot(xa, w1a_ref[0], preferred_element_type=F32)
    b = jnp.dot(xb, w1b_ref[0], preferred_element_type=F32)
    return _gelu(a + pltpu.roll(b, N_CMP_PAD - 1, axis=0)).astype(BF16)


def _cmp_kernel(xk_ref, xv_ref, posa_ref, posb_ref, w1a_ref, w1b_ref, w2k_ref, w2vt_ref, posc_ref, kc_ref, vct_ref):
    hk = _cmp_hidden(xk_ref, posa_ref.at[0:1], posb_ref.at[0:1], w1a_ref.at[0:1], w1b_ref.at[0:1])
    kc_ref[0, 0] = (jnp.dot(hk, w2k_ref[...], preferred_element_type=F32) + posc_ref[...]).astype(BF16)
    hv = _cmp_hidden(xv_ref, posa_ref.at[1:2], posb_ref.at[1:2], w1a_ref.at[1:2], w1b_ref.at[1:2])
    vct = lax.dot_general(w2vt_ref[...], hv, (((1,), (1,)), ((), ())), preferred_element_type=F32)
    n = lax.broadcasted_iota(jnp.int32, vct.shape, 1)
    vct_ref[0, 0] = jnp.where(n < N_CMP, vct, 0.0).astype(BF16)


def _pos_features(pos_hi, pos_lo, n):
    feat = jnp.stack([pos_hi] * 3 + [pos_lo] * 3, axis=1).astype(F32)
    return jnp.pad(feat, ((0, 0), (AUG_POS0, AUG - AUG_POS0 - 6)))


def _compress(kvc, pos_k, pos_v, k_w1, k_w2, v_w1, v_w2):
    n_chunk = SEQ // CMP_STRIDE
    x = (kvc.reshape(BATCH, n_chunk, CMP_STRIDE, 2 * N_KV_GROUPS, HEAD_DIM)
         .transpose(0, 3, 1, 2, 4).reshape(BATCH, 2 * N_KV_GROUPS, n_chunk, CHUNK_FEAT))
    pos = jnp.stack([pos_k, pos_v])
    posa = pos[:, :CMP_STRIDE].reshape(2, 1, CHUNK_FEAT)
    posb = pos[:, CMP_STRIDE:].reshape(2, 1, CHUNK_FEAT)
    w1 = jnp.stack([k_w1, v_w1]).astype(BF16)
    w1a = w1[:, :CMP_STRIDE].reshape(2, CHUNK_FEAT, CMP_HIDDEN)
    w1b = w1[:, CMP_STRIDE:].reshape(2, CHUNK_FEAT, CMP_HIDDEN)
    w2k = jnp.pad(k_w2, ((0, 0), (0, AUG - HEAD_DIM))).astype(BF16)
    w2vt = v_w2.T.astype(BF16)
    nb = jnp.arange(N_CMP_PAD)
    posc = _pos_features(nb // 4, (nb % 4) * CMP_STRIDE + 0.5 * (CMP_BLOCK - 1), N_CMP_PAD)
    full = lambda shape: pl.BlockSpec(shape, lambda b, g: (0,) * len(shape))
    return pl.pallas_call(
        _cmp_kernel,
        grid=(BATCH, N_KV_GROUPS),
        in_specs=[
            pl.BlockSpec((1, 1, n_chunk, CHUNK_FEAT), lambda b, g: (b, g, 0, 0)),
            pl.BlockSpec((1, 1, n_chunk, CHUNK_FEAT), lambda b, g: (b, N_KV_GROUPS + g, 0, 0)),
            full((2, 1, CHUNK_FEAT)),
            full((2, 1, CHUNK_FEAT)),
            full((2, CHUNK_FEAT, CMP_HIDDEN)),
            full((2, CHUNK_FEAT, CMP_HIDDEN)),
            full((CMP_HIDDEN, AUG)),
            full((HEAD_DIM, CMP_HIDDEN)),
            full((N_CMP_PAD, AUG)),
        ],
        out_specs=[
            pl.BlockSpec((1, 1, N_CMP_PAD, AUG), lambda b, g: (b, g, 0, 0)),
            pl.BlockSpec((1, 1, HEAD_DIM, N_CMP_PAD), lambda b, g: (b, g, 0, 0)),
        ],
        out_shape=[
            jax.ShapeDtypeStruct((BATCH, N_KV_GROUPS, N_CMP_PAD, AUG), BF16),
            jax.ShapeDtypeStruct((BATCH, N_KV_GROUPS, HEAD_DIM, N_CMP_PAD), BF16),
        ],
        compiler_params=_cparams(2),
        name="nsa_compress",
    )(x, x, posa, posb, w1a, w1b, w2k, w2vt, posc)


def _select_top_n(score):
    n_sub = N_SEL_BLOCKS // SUBLANES
    pieces = [score[SUBLANES * v:SUBLANES * (v + 1), :] for v in range(n_sub)]
    ranks = [jnp.zeros((SUBLANES, TQ), F32) for _ in range(n_sub)]
    sub = lax.broadcasted_iota(jnp.int32, (SUBLANES, TQ), 0)
    for j2 in range(N_SEL_BLOCKS):
        v2, u2 = divmod(j2, SUBLANES)
        other = jnp.broadcast_to(score[j2:j2 + 1, :], (SUBLANES, TQ))
        for v in range(n_sub):
            if v < v2:
                beats = other > pieces[v]
            elif v > v2:
                beats = other >= pieces[v]
            else:
                beats = (other > pieces[v]) | ((other == pieces[v]) & (sub > u2))
            ranks[v] = ranks[v] + jnp.where(beats, 1.0, 0.0)
    return jnp.concatenate(ranks, axis=0) < N_SELECT


def _attn_kernel(qt_ref, kk_ref, vt_ref, kc_ref, vct_ref, gt_ref, rs_ref, rw_ref, consts_ref, constw_ref,
                 ovlt_ref, fixed_ref, cmpd_ref, diagd_ref, wmask_ref,
                 o_ref, ksa_ref, kwa_ref, vst_ref, vwt_ref, wq_ref, m_ref, acc_ref):
    qi = pl.program_id(2)
    q0 = qi * TQ2
    n_vt = SEQ // LANES
    n_wpad = WINDOW // LANES

    @pl.when(qi == 0)
    def _():
        rows = 512

        def fill(i, carry):
            r0 = pl.multiple_of(i * rows, rows)
            kk = kk_ref[0, pl.ds(r0, rows), :]
            ksa_ref[pl.ds(r0, rows), :] = (jnp.dot(kk, rs_ref[...], preferred_element_type=F32)
                                           + consts_ref[pl.ds(r0, rows), :].astype(F32)).astype(BF16)
            kwa_ref[pl.ds(WINDOW + r0, rows), :] = (jnp.dot(kk, rw_ref[...], preferred_element_type=F32)
                                                    + constw_ref[pl.ds(r0, rows), :].astype(F32)).astype(BF16)
            return carry

        lax.fori_loop(0, SEQ // rows, fill, 0)
        kwa_ref[0:WINDOW, :] = (lax.broadcasted_iota(jnp.int32, (WINDOW, AUG), 1) == AUG_PAD).astype(BF16)
        ones_pad = (lax.broadcasted_iota(jnp.int32, (n_vt, V_ROWS - HEAD_DIM, LANES), 1) == 0).astype(BF16)
        vst_ref[:, 0:HEAD_DIM, :] = vt_ref[0, :, 0:HEAD_DIM, :]
        vst_ref[:, HEAD_DIM:, :] = ones_pad
        vwt_ref[0:n_wpad] = jnp.zeros((n_wpad, V_ROWS, LANES), BF16)
        vwt_ref[n_wpad:, 0:HEAD_DIM, :] = vt_ref[0, :, HEAD_DIM:, :]
        vwt_ref[n_wpad:, HEAD_DIM:, :] = ones_pad
        wq_ref[AUG_SEL0:AUG_SEL0 + N_SEL_BLOCKS, :] = jnp.zeros((N_SEL_BLOCKS, N_CHAIN * HQ), BF16)
        wq_ref[AUG_POS0:AUG_POS0 + AUG_FIXED_ROWS, :] = fixed_ref[0]
        wq_ref[AUG_POS0 + AUG_FIXED_ROWS:, :] = jnp.zeros((AUG - AUG_POS0 - AUG_FIXED_ROWS, N_CHAIN * HQ), BF16)

    qt = qt_ref[0]
    for c in range(N_CHAIN):
        for r in range(HEADS_PER_GROUP):
            lane0 = c * HQ + r * TQ
            wq_ref[0:HEAD_DIM, lane0:lane0 + TQ] = qt[r * HEAD_DIM:(r + 1) * HEAD_DIM, c * TQ:(c + 1) * TQ]

    def wq_chain(c):
        return wq_ref[:, c * HQ:(c + 1) * HQ]

    lane_q = lax.broadcasted_iota(jnp.int32, (1, HQ), 1) & (TQ - 1)

    def run_jobs(jobs, ahead=2):
        scores = [None] * len(jobs)
        for i in range(min(ahead, len(jobs))):
            scores[i] = jobs[i][0]()
        for i, (_, soft, value) in enumerate(jobs):
            mid = soft(scores[i])
            if i + ahead < len(jobs):
                scores[i + ahead] = jobs[i + ahead][0]()
            value(mid)

    def v_tiles(v_ref, row0, tk):
        vt0 = row0 >> _log2(LANES)
        return jnp.concatenate([v_ref[vt0 + j] for j in range(tk // LANES)], axis=1)

    oc_t = [None] * N_CHAIN
    imp_t = [None] * N_CHAIN

    def cmp_job(c):
        q0c = q0 + c * TQ

        def score():
            return jnp.dot(kc_ref[0, 0], wq_chain(c), preferred_element_type=F32)

        def soft(s):
            s = jnp.where(cmpd_ref[...] <= q0c, s, NEG_INF)
            p = jnp.exp2(s - jnp.max(s, axis=0, keepdims=True))
            has_c = (q0c + lane_q >= CMP_BLOCK - 1).astype(F32)
            p = p * (has_c / jnp.sum(p, axis=0, keepdims=True))
            psum = p[:, 0:TQ] + p[:, TQ:2 * TQ] + p[:, 2 * TQ:3 * TQ] + p[:, 3 * TQ:4 * TQ]
            return p.astype(BF16), _split3(psum)

        def value(mid):
            pb, (p1, p2, p3) = mid
            oc_t[c] = jnp.dot(vct_ref[0, 0], pb, preferred_element_type=F32)
            ovlt = ovlt_ref[...]
            imp_t[c] = (jnp.dot(ovlt, p1, preferred_element_type=F32)
                        + jnp.dot(ovlt, p2, preferred_element_type=F32)
                        + jnp.dot(ovlt, p3, preferred_element_type=F32))

        return score, soft, value

    ow_t = [None] * N_CHAIN

    def win_job(c):
        row0 = pl.multiple_of(q0 + c * TQ, LANES)

        def score():
            return jnp.dot(kwa_ref[pl.ds(row0, TKW), :], wq_chain(c), preferred_element_type=F32)

        def soft(sc):
            sc = sc + wmask_ref[...]
            return jnp.exp2(sc - jnp.max(sc, axis=0, keepdims=True)).astype(BF16)

        def value(pt):
            acc = jnp.dot(v_tiles(vwt_ref, row0, TKW), pt, preferred_element_type=F32)
            ow_t[c] = acc[0:HEAD_DIM] / acc[HEAD_DIM:HEAD_DIM + 1]

        return score, soft, value

    run_jobs([cmp_job(c) for c in range(N_CHAIN)] + [win_job(c) for c in range(N_CHAIN)])

    sel_bias = []
    for c in range(N_CHAIN):
        q0c = q0 + c * TQ
        jb = lax.broadcasted_iota(jnp.int32, (N_SEL_BLOCKS, TQ), 0)
        bt = (q0c + lax.broadcasted_iota(jnp.int32, (N_SEL_BLOCKS, TQ), 1)) >> _log2(SEL_BLOCK)
        forced = (jb == 0) | (jb == bt) | (jb == bt - 1)
        score = jnp.where(forced, FORCE_SCORE, jnp.where(jb <= bt, imp_t[c], -FORCE_SCORE))
        sel_bias.append(jnp.where(_select_top_n(score), 0.0, NEG_INF).astype(BF16))
    for c in range(N_CHAIN):
        for r in range(HEADS_PER_GROUP):
            lane0 = c * HQ + r * TQ
            wq_ref[AUG_SEL0:AUG_SEL0 + N_SEL_BLOCKS, lane0:lane0 + TQ] = sel_bias[c]

    m_ref[...] = jnp.full_like(m_ref, NEG_INF)
    acc_ref[...] = jnp.zeros_like(acc_ref)

    def sel_job(c, row0, tk, diagonal):
        row0 = pl.multiple_of(row0, LANES)

        def score():
            return jnp.dot(ksa_ref[pl.ds(row0, tk), :], wq_chain(c), preferred_element_type=F32)

        def soft(sc):
            if diagonal:
                sc = jnp.where(diagd_ref[0:tk, :] <= TQ * c, sc, NEG_INF)
            m_old = m_ref[c]
            m_new = jnp.maximum(m_old, jnp.max(sc, axis=0, keepdims=True))
            m_ref[c] = m_new
            return jnp.exp2(sc - m_new).astype(BF16), jnp.exp2(m_old - m_new)

        def value(mid):
            pt, alpha = mid
            acc_ref[c] = alpha * acc_ref[c] + jnp.dot(v_tiles(vst_ref, row0, tk), pt, preferred_element_type=F32)

        return score, soft, value

    n_full = q0 >> _log2(TKS)

    def tile_jobs(tile):
        return [sel_job(c, tile * TKS + s * TKJ, TKJ, False) for s in range(TKS // TKJ) for c in range(N_CHAIN)]

    def sel_pair(kt2, carry):
        run_jobs(tile_jobs(2 * kt2) + tile_jobs(2 * kt2 + 1), ahead=4)
        return carry

    lax.fori_loop(0, n_full >> 1, sel_pair, 0)

    def end_jobs(has_odd):
        jobs = tile_jobs(n_full - 1) if has_odd else []
        return jobs + [sel_job(c, q0, TQ * (c + 1), True) for c in range(N_CHAIN)]

    for has_odd in (False, True):
        @pl.when((n_full & 1) == int(has_odd))
        def _():
            run_jobs(end_jobs(has_odd), ahead=3)

    os_t = []
    for c in range(N_CHAIN):
        acc = acc_ref[c]
        os_t.append(acc[0:HEAD_DIM] / acc[HEAD_DIM:HEAD_DIM + 1])

    gt = gt_ref[0]
    cols = []
    for c in range(N_CHAIN):
        blocks = []
        for r in range(HEADS_PER_GROUP):
            sl = slice(r * TQ, (r + 1) * TQ)
            g3 = gt[3 * r:3 * r + 3, c * TQ:(c + 1) * TQ]
            blocks.append(g3[0:1] * oc_t[c][:, sl] + g3[1:2] * os_t[c][:, sl] + g3[2:3] * ow_t[c][:, sl])
        cols.append(jnp.concatenate(blocks, axis=0))
    o_ref[0] = jnp.concatenate(cols, axis=1).T.astype(BF16)


def _attention(qt, kk, vt, kc, vct, gt):
    hw = HEADS_PER_GROUP * HEAD_DIM
    src = jnp.arange(LANES)
    col = jnp.arange(AUG)
    rs = (src[:, None] == col[None, :]).astype(BF16) * (src[:, None] < HEAD_DIM)
    rw = (src[:, None] - HEAD_DIM == col[None, :]).astype(BF16) * (src[:, None] >= HEAD_DIM)
    key = jnp.arange(SEQ)
    pos = _pos_features(key // SEL_BLOCK, key % SEL_BLOCK, SEQ)
    onehot = (col[None, :] - AUG_SEL0 == (key // SEL_BLOCK)[:, None]).astype(F32)
    consts = (pos + onehot).astype(BF16)
    constw = pos.astype(BF16)
    cmp_start = jnp.arange(N_CMP_PAD) * CMP_STRIDE
    sel_start = jnp.arange(N_SEL_BLOCKS) * SEL_BLOCK
    ovlt = ((cmp_start[None, :] <= sel_start[:, None] + SEL_BLOCK - 1)
            & (cmp_start[None, :] + CMP_BLOCK - 1 >= sel_start[:, None])
            & (jnp.arange(N_CMP_PAD)[None, :] < N_CMP)).astype(BF16)
    hh = jnp.arange(1, N_HEADS + 1, dtype=F32)
    slope = jnp.exp2(-8.0 * hh / N_HEADS) * LOG2E
    s1, s2, s3 = _split3(slope)
    parts = jnp.stack([s1, s2, s3]).astype(F32)
    rows = jnp.concatenate([parts * SEL_BLOCK, parts,
                            jnp.zeros((AUG_POS_ROWS - 6, N_HEADS), F32)])
    slt = jnp.repeat(rows.reshape(AUG_POS_ROWS, N_KV_GROUPS, HEADS_PER_GROUP).transpose(1, 0, 2),
                     TQ, axis=2)
    pad_rows = jnp.zeros((N_KV_GROUPS, AUG_POS_ROWS, HQ), F32).at[:, 0, :].set(NEG_INF)
    fixed = jnp.tile(jnp.concatenate([slt, pad_rows], axis=1), (1, 1, N_CHAIN)).astype(BF16)
    lane_i = jnp.arange(HQ) % TQ
    nb = jnp.arange(N_CMP_PAD)
    cmpd = jnp.where(nb[:, None] < N_CMP, nb[:, None] * CMP_STRIDE + (CMP_BLOCK - 1) - lane_i[None, :],
                     2 * SEQ).astype(jnp.int32)

    def key_mask(n_keys, lo, hi):
        d = jnp.arange(n_keys)[:, None] - lane_i[None, :]
        return jnp.where((d >= lo) & (d <= hi), 0.0, NEG_INF).astype(F32)

    diagd = (jnp.arange(TQ2)[:, None] - lane_i[None, :]).astype(jnp.int32)
    wmask = key_mask(TKW, 1, WINDOW)

    nq = SEQ // TQ2
    n_vt = SEQ // LANES
    const = lambda shape: pl.BlockSpec(shape, lambda b, g, i: (0,) * len(shape))
    return pl.pallas_call(
        _attn_kernel,
        grid=(BATCH, N_KV_GROUPS, nq),
        in_specs=[
            pl.BlockSpec((1, hw, TQ2), lambda b, g, i: (b, g, i)),
            pl.BlockSpec((1, SEQ, LANES), lambda b, g, i: (b, 0, g)),
            pl.BlockSpec((1, n_vt, LANES, LANES), lambda b, g, i: (b, 0, g, 0)),
            pl.BlockSpec((1, 1, N_CMP_PAD, AUG), lambda b, g, i: (b, g, 0, 0)),
            pl.BlockSpec((1, 1, HEAD_DIM, N_CMP_PAD), lambda b, g, i: (b, g, 0, 0)),
            pl.BlockSpec((1, GATE_ROWS, TQ2), lambda b, g, i: (b, g, i)),
            const((LANES, AUG)),
            const((LANES, AUG)),
            const((SEQ, AUG)),
            const((SEQ, AUG)),
            const((N_SEL_BLOCKS, N_CMP_PAD)),
            pl.BlockSpec((1, AUG_FIXED_ROWS, N_CHAIN * HQ), lambda b, g, i: (g, 0, 0)),
            const((N_CMP_PAD, HQ)),
            const((TQ2, HQ)),
            const((TKW, HQ)),
        ],
        out_specs=pl.BlockSpec((1, TQ2, hw), lambda b, g, i: (b, i, g)),
        out_shape=jax.ShapeDtypeStruct((BATCH, SEQ, D_MODEL), BF16),
        scratch_shapes=[
            pltpu.VMEM((SEQ, AUG), BF16),
            pltpu.VMEM((WINDOW + SEQ, AUG), BF16),
            pltpu.VMEM((n_vt, V_ROWS, LANES), BF16),
            pltpu.VMEM((WINDOW // LANES + n_vt, V_ROWS, LANES), BF16),
            pltpu.VMEM((AUG, N_CHAIN * HQ), BF16),
            pltpu.VMEM((N_CHAIN, 1, HQ), F32),
            pltpu.VMEM((N_CHAIN, V_ROWS, HQ), F32),
        ],
        compiler_params=_cparams(3),
        name="nsa_attention",
    )(qt, kk, vt, kc, vct, gt, rs, rw, consts, constw, ovlt, fixed, cmpd, diagd, wmask)


def _out_kernel(o_ref, x_ref, mod_ref, w_ref, lng_ref, lnb_ref, y_ref):
    y = jnp.dot(o_ref[0], w_ref[...], preferred_element_type=F32)
    gate = mod_ref[0][2:3]
    y_ref[0] = _deepnorm_ln(x_ref[0], gate, y, lng_ref[...], lnb_ref[...])


def _out_proj(o, x, mod, w_out, ln_g, ln_b):
    row = lambda b, t: (b, t, 0)
    c2 = lambda b, t: (0, 0)
    return pl.pallas_call(
        _out_kernel,
        grid=(BATCH, SEQ // TM),
        in_specs=[
            pl.BlockSpec((1, TM, D_MODEL), row),
            pl.BlockSpec((1, TM, D_MODEL), row),
            pl.BlockSpec((1, 6, D_MODEL), lambda b, t: (b, 0, 0)),
            pl.BlockSpec((D_MODEL, D_MODEL), c2),
            pl.BlockSpec((1, D_MODEL), c2),
            pl.BlockSpec((1, D_MODEL), c2),
        ],
        out_specs=pl.BlockSpec((1, TM, D_MODEL), row),
        out_shape=jax.ShapeDtypeStruct((BATCH, SEQ, D_MODEL), F32),
        compiler_params=_cparams(2),
        name="nsa_out_proj",
    )(o, x, mod, w_out.astype(BF16), ln_g.reshape(1, D_MODEL), ln_b.reshape(1, D_MODEL))


def _nsa_layer(x, mod, w_in, pos_k, pos_v, k_w1, k_w2, v_w1, v_w2, w_out, ln_g, ln_b):
    qt, kk, vt, gt, kvc = _nsa_proj(x, mod, w_in)
    kc, vct = _compress(kvc, pos_k, pos_v, k_w1, k_w2, v_w1, v_w2)
    o = _attention(qt, kk, vt, kc, vct, gt)
    return _out_proj(o, x, mod, w_out, ln_g, ln_b)


def kernel(x, c, ada_w, ada_b, ln1_g, ln1_b, ln2_g, ln2_b, ffn_w_in, ffn_conv, ffn_w_out, conv_w_in, conv_w,
           conv_w_out, pool_w_in, pool_w_grp, pool_scale, pool_w_out, nsa_w_in, nsa_cmp_pos_k, nsa_cmp_pos_v,
           nsa_cmp_k_w1, nsa_cmp_k_w2, nsa_cmp_v_w1, nsa_cmp_v_w2, nsa_w_out):
    assert x.shape == (BATCH, SEQ, D_MODEL) and x.dtype == F32
    mods = _modulation(c, ada_w, ada_b).reshape(DEPTH, BATCH, 6, D_MODEL)
    for i in range(DEPTH):
        mod = mods[i]
        m, j = i % N_MIXERS, i // N_MIXERS
        if m == 0:
            x = _gated_layer("conv", x, mod, 0, conv_w_in[j], conv_w[j], conv_w_out[j], ln1_g[i], ln1_b[i])
        elif m == 1:
            x = _pool_layer(x, mod, pool_w_in[j], pool_w_grp[j], pool_scale[j], pool_w_out[j], ln1_g[i], ln1_b[i])
        else:
            x = _nsa_layer(x, mod, nsa_w_in[j], nsa_cmp_pos_k[j], nsa_cmp_pos_v[j], nsa_cmp_k_w1[j],
                           nsa_cmp_k_w2[j], nsa_cmp_v_w1[j], nsa_cmp_v_w2[j], nsa_w_out[j], ln1_g[i], ln1_b[i])
        x = _gated_layer("ffn", x, mod, 3, ffn_w_in[i], ffn_conv[i], ffn_w_out[i], ln2_g[i], ln2_b[i])
    return x
```

```python
import functools
import math

import jax
import jax.numpy as jnp
from jax import lax
from jax.experimental import pallas as pl
from jax.experimental.pallas import tpu as pltpu

F32 = jnp.float32
BF16 = jnp.bfloat16

D_MODEL = 1024
BATCH = 16
SEQ = 4096
DEPTH = 4
N_MIXERS = 3
CONV_WIDTH = 3
D_FF = 2816
POOL_WINDOWS = (2, 4, 8, 16)
N_POOL_GROUPS = 4
POOL_GROUP = D_MODEL // N_POOL_GROUPS
N_HEADS = 16
HEAD_DIM = 64
N_KV_GROUPS = 4
HEADS_PER_GROUP = 4
KV_WIDTH = N_KV_GROUPS * HEAD_DIM
CMP_BLOCK = 32
CMP_STRIDE = 16
CMP_HIDDEN = 256
SEL_BLOCK = 64
N_SELECT = 16
WINDOW = 512
N_BRANCH = 3
ALPHA = (2.0 * DEPTH) ** 0.25
LN_EPS = 1e-5
NEG_INF = -1e30
FORCE_SCORE = 1e4
LOG2E = math.log2(math.e)

N_CMP = (SEQ - CMP_BLOCK) // CMP_STRIDE + 1
N_CMP_PAD = 256
N_SEL_BLOCKS = SEQ // SEL_BLOCK
LANES = 128
SUBLANES = 8

TM = 512
FC = 256
CONV_HALO = 8
POOL_HALO = 16
TQ = 128
N_CHAIN = 4
TQ2 = N_CHAIN * TQ
HQ = HEADS_PER_GROUP * TQ
TKS = 512
TKJ = 256
TKW = WINDOW + TQ
VMEM_LIMIT = 56 * 1024 * 1024

AUG = 256
AUG_SEL0 = HEAD_DIM
AUG_POS0 = 2 * HEAD_DIM
AUG_POS_ROWS = 16
AUG_PAD = AUG_POS0 + AUG_POS_ROWS
AUG_FIXED_ROWS = 2 * AUG_POS_ROWS
V_ROWS = 80
GATE_ROWS = 16


def _cparams(n_axes):
    return pltpu.CompilerParams(dimension_semantics=("arbitrary",) * n_axes,
                                vmem_limit_bytes=VMEM_LIMIT)


def _log2(n):
    assert n & (n - 1) == 0
    return n.bit_length() - 1


def _gelu(x):
    c = math.sqrt(2.0 / math.pi)
    return x * (0.5 * (1.0 + jnp.tanh(c * (x + 0.044715 * (x * x * x)))))


def _deepnorm_ln(x, gate, y, ln_g, ln_b):
    r = ALPHA * x + (1.0 + gate) * y
    mu = jnp.mean(r, axis=-1, keepdims=True)
    d = r - mu
    var = jnp.mean(d * d, axis=-1, keepdims=True)
    return d * lax.rsqrt(var + LN_EPS) * ln_g + ln_b


def _split3(a):
    a1 = a.astype(BF16)
    r1 = a - a1.astype(F32)
    a2 = r1.astype(BF16)
    a3 = (r1 - a2.astype(F32)).astype(BF16)
    return a1, a2, a3


def _ada_kernel(c_ref, w_ref, b_ref, o_ref):
    c = c_ref[...]
    cond = c * jax.nn.sigmoid(c)
    c1, c2, c3 = _split3(cond)
    w1, w2, w3 = _split3(w_ref[0])
    acc = jnp.dot(c1, w1, preferred_element_type=F32)
    acc += jnp.dot(c1, w2, preferred_element_type=F32)
    acc += jnp.dot(c2, w1, preferred_element_type=F32)
    acc += jnp.dot(c1, w3, preferred_element_type=F32)
    acc += jnp.dot(c2, w2, preferred_element_type=F32)
    acc += jnp.dot(c3, w1, preferred_element_type=F32)
    o_ref[0] = acc + b_ref[0]


def _modulation(c, ada_w, ada_b):
    nc = 1536
    n_col = 6 * D_MODEL // nc
    return pl.pallas_call(
        _ada_kernel,
        grid=(DEPTH, n_col),
        in_specs=[
            pl.BlockSpec((BATCH, D_MODEL), lambda i, j: (0, 0)),
            pl.BlockSpec((1, D_MODEL, nc), lambda i, j: (i, 0, j)),
            pl.BlockSpec((1, 1, nc), lambda i, j: (i, 0, j)),
        ],
        out_specs=pl.BlockSpec((1, BATCH, nc), lambda i, j: (i, 0, j)),
        out_shape=jax.ShapeDtypeStruct((DEPTH, BATCH, 6 * D_MODEL), F32),
        compiler_params=_cparams(2),
        name="modulation",
    )(c, ada_w, ada_b.reshape(DEPTH, 1, 6 * D_MODEL))


def _gated_kernel(mode, n_chunks, mrow, x_ref, mod_ref, win_ref, wconv_ref, wout_ref, lng_ref, lnb_ref,
                  o_ref, h_ref, carry_ref, cbuf_ref, acc_ref):
    t = pl.program_id(1)

    @pl.when(t == 0)
    def _():
        carry_ref[...] = jnp.zeros_like(carry_ref)

    m = mod_ref[0]
    sh, sc, gate = m[mrow:mrow + 1], m[mrow + 1:mrow + 2], m[mrow + 2:mrow + 3]
    feat = n_chunks * FC
    n_parts = 3 if mode == "conv" else 2

    xt = x_ref[0]
    h_ref[...] = (xt * (1.0 + sc) + sh).astype(BF16)

    def in_proj(j):
        h = h_ref[...]
        return [jnp.dot(h, win_ref[:, p * feat + j * FC:p * feat + (j + 1) * FC], preferred_element_type=F32)
                for p in range(n_parts)]

    p_next = in_proj(0)
    for j in range(n_chunks):
        p = p_next
        if mode == "conv":
            gate_in = p[0]
            cin = p[1] * p[2]
        else:
            cin, gate_in = p
        cbuf_ref[0:CONV_HALO, :] = carry_ref[j]
        cbuf_ref[CONV_HALO:, :] = cin
        carry_ref[j] = cin[TM - CONV_HALO:, :]
        w = wconv_ref[j]
        y = (w[0:1] * cbuf_ref[CONV_HALO - 2:CONV_HALO - 2 + TM, :]
             + w[1:2] * cbuf_ref[CONV_HALO - 1:CONV_HALO - 1 + TM, :]
             + w[2:3] * cin)
        if mode == "conv":
            z = gate_in * y
        else:
            z = _gelu(y) * gate_in
        z = z.astype(BF16)
        if j + 1 < n_chunks:
            p_next = in_proj(j + 1)
        out = jnp.dot(z, wout_ref[j], preferred_element_type=F32)
        if j == 0:
            acc_ref[...] = out
        else:
            acc_ref[...] += out

    o_ref[0] = _deepnorm_ln(xt, gate, acc_ref[...], lng_ref[...], lnb_ref[...])


def _gated_layer(mode, x, mod, mrow, w_in, w_conv, w_out, ln_g, ln_b):
    feat = w_out.shape[0]
    n_parts = w_in.shape[1] // feat
    n_chunks = feat // FC
    assert n_parts == (3 if mode == "conv" else 2)
    wout_r = w_out.astype(BF16).reshape(n_chunks, FC, D_MODEL)
    wconv_r = w_conv.reshape(CONV_WIDTH, n_chunks, FC).transpose(1, 0, 2)
    const3 = lambda b, t: (0, 0, 0)
    return pl.pallas_call(
        functools.partial(_gated_kernel, mode, n_chunks, mrow),
        grid=(BATCH, SEQ // TM),
        in_specs=[
            pl.BlockSpec((1, TM, D_MODEL), lambda b, t: (b, t, 0)),
            pl.BlockSpec((1, 6, D_MODEL), lambda b, t: (b, 0, 0)),
            pl.BlockSpec((D_MODEL, n_parts * feat), lambda b, t: (0, 0)),
            pl.BlockSpec((n_chunks, CONV_WIDTH, FC), const3),
            pl.BlockSpec((n_chunks, FC, D_MODEL), const3),
            pl.BlockSpec((1, D_MODEL), lambda b, t: (0, 0)),
            pl.BlockSpec((1, D_MODEL), lambda b, t: (0, 0)),
        ],
        out_specs=pl.BlockSpec((1, TM, D_MODEL), lambda b, t: (b, t, 0)),
        out_shape=jax.ShapeDtypeStruct((BATCH, SEQ, D_MODEL), F32),
        scratch_shapes=[
            pltpu.VMEM((TM, D_MODEL), BF16),
            pltpu.VMEM((n_chunks, CONV_HALO, FC), F32),
            pltpu.VMEM((CONV_HALO + TM, FC), F32),
            pltpu.VMEM((TM, D_MODEL), F32),
        ],
        compiler_params=_cparams(2),
        name="gated_" + mode,
    )(x, mod, w_in.astype(BF16), wconv_r, wout_r, ln_g.reshape(1, D_MODEL), ln_b.reshape(1, D_MODEL))


def _pool_kernel(x_ref, mod_ref, win_ref, wgrp_ref, scale_ref, wout_ref, lng_ref, lnb_ref,
                 o_ref, carry_ref, ubuf_ref, z_ref):
    t = pl.program_id(1)

    @pl.when(t == 0)
    def _():
        carry_ref[...] = jnp.zeros_like(carry_ref)

    xt = x_ref[0]
    m = mod_ref[0]
    sh, sc, gate = m[0:1], m[1:2], m[2:3]
    h = (xt * (1.0 + sc) + sh).astype(BF16)
    u = jnp.dot(h, win_ref[...], preferred_element_type=F32)
    ubuf_ref[0:POOL_HALO, :] = carry_ref[...]
    ubuf_ref[POOL_HALO:, :] = u
    carry_ref[...] = u[TM - POOL_HALO:, :]
    pos = t * TM + lax.broadcasted_iota(jnp.int32, (TM, 1), 0)
    for g, w in enumerate(POOL_WINDOWS):
        lo, hi = g * POOL_GROUP, (g + 1) * POOL_GROUP
        e = ubuf_ref[:, lo:hi]
        s = e
        k = 1
        while k < w:
            s = s + pltpu.roll(s, k, axis=0)
            k *= 2
        cnt = jnp.minimum(pos + 1, w).astype(F32)
        pooled = s[POOL_HALO:, :] / cnt - e[POOL_HALO:, :]
        zg = jnp.dot(pooled.astype(BF16), wgrp_ref[g], preferred_element_type=F32)
        z_ref[:, lo:hi] = (zg * scale_ref[:, lo:hi]).astype(BF16)
    y = jnp.dot(z_ref[...], wout_ref[...], preferred_element_type=F32)
    o_ref[0] = _deepnorm_ln(xt, gate, y, lng_ref[...], lnb_ref[...])


def _pool_layer(x, mod, w_in, w_grp, scale, w_out, ln_g, ln_b):
    c2 = lambda b, t: (0, 0)
    return pl.pallas_call(
        _pool_kernel,
        grid=(BATCH, SEQ // TM),
        in_specs=[
            pl.BlockSpec((1, TM, D_MODEL), lambda b, t: (b, t, 0)),
            pl.BlockSpec((1, 6, D_MODEL), lambda b, t: (b, 0, 0)),
            pl.BlockSpec((D_MODEL, D_MODEL), c2),
            pl.BlockSpec((N_POOL_GROUPS, POOL_GROUP, POOL_GROUP), lambda b, t: (0, 0, 0)),
            pl.BlockSpec((1, D_MODEL), c2),
            pl.BlockSpec((D_MODEL, D_MODEL), c2),
            pl.BlockSpec((1, D_MODEL), c2),
            pl.BlockSpec((1, D_MODEL), c2),
        ],
        out_specs=pl.BlockSpec((1, TM, D_MODEL), lambda b, t: (b, t, 0)),
        out_shape=jax.ShapeDtypeStruct((BATCH, SEQ, D_MODEL), F32),
        scratch_shapes=[
            pltpu.VMEM((POOL_HALO, D_MODEL), F32),
            pltpu.VMEM((POOL_HALO + TM, D_MODEL), F32),
            pltpu.VMEM((TM, D_MODEL), BF16),
        ],
        compiler_params=_cparams(2),
        name="pool_mixer",
    )(x, mod, w_in.astype(BF16), w_grp.astype(BF16), scale.reshape(1, D_MODEL), w_out.astype(BF16),
      ln_g.reshape(1, D_MODEL), ln_b.reshape(1, D_MODEL))


NSA_T_ROWS = D_MODEL + 2 * KV_WIDTH + N_KV_GROUPS * GATE_ROWS
Q_SCALE = HEAD_DIM ** -0.5 * LOG2E


def _nsa_proj_kernel(x_ref, mod_ref, wn_ref, wt_ref, qt_ref, kk_ref, vt_ref, gt_ref, kvc_ref):
    xt = x_ref[0]
    m = mod_ref[0]
    sh, sc = m[0:1], m[1:2]
    h = (xt * (1.0 + sc) + sh).astype(BF16)
    rn = jnp.dot(h, wn_ref[...], preferred_element_type=F32)
    kk_ref[0] = rn[:, :2 * KV_WIDTH].astype(BF16)
    kvc_ref[0] = rn[:, 2 * KV_WIDTH:]
    rt = lax.dot_general(wt_ref[...], h, (((1,), (1,)), ((), ())), preferred_element_type=F32)
    qt_ref[0] = (rt[0:D_MODEL] * Q_SCALE).astype(BF16)
    v_t = rt[D_MODEL:D_MODEL + 2 * KV_WIDTH].astype(BF16)
    for j in range(TM // LANES):
        vt_ref[0, j] = v_t[:, j * LANES:(j + 1) * LANES]
    gt_ref[0] = jax.nn.sigmoid(rt[D_MODEL + 2 * KV_WIDTH:])


def _nsa_proj(x, mod, w_in):
    kc0 = D_MODEL
    vc0, ks0, vs0, kw0, vw0, gl0 = (kc0 + i * KV_WIDTH for i in range(1, 7))
    grp = lambda c0, g: jnp.arange(c0 + g * HEAD_DIM, c0 + (g + 1) * HEAD_DIM)
    cols_n = jnp.concatenate([jnp.concatenate([grp(ks0, g), grp(kw0, g)]) for g in range(N_KV_GROUPS)]
                             + [jnp.arange(kc0, ks0)])
    wn = jnp.take(w_in, cols_n, axis=1).astype(BF16)
    cols_v = jnp.concatenate([jnp.concatenate([grp(vs0, g), grp(vw0, g)]) for g in range(N_KV_GROUPS)])
    n_gate = HEADS_PER_GROUP * N_BRANCH
    w_gate = w_in[:, gl0:gl0 + N_KV_GROUPS * n_gate].reshape(D_MODEL, N_KV_GROUPS, n_gate)
    w_gate = jnp.pad(w_gate, ((0, 0), (0, 0), (0, GATE_ROWS - n_gate))).reshape(D_MODEL, N_KV_GROUPS * GATE_ROWS)
    wt = jnp.concatenate([w_in[:, :D_MODEL], jnp.take(w_in, cols_v, axis=1), w_gate], axis=1).T.astype(BF16)
    row = lambda b, t: (b, t, 0)
    colt = lambda b, t: (b, 0, t)
    return pl.pallas_call(
        _nsa_proj_kernel,
        grid=(BATCH, SEQ // TM),
        in_specs=[
            pl.BlockSpec((1, TM, D_MODEL), row),
            pl.BlockSpec((1, 6, D_MODEL), lambda b, t: (b, 0, 0)),
            pl.BlockSpec((D_MODEL, 4 * KV_WIDTH), lambda b, t: (0, 0)),
            pl.BlockSpec((NSA_T_ROWS, D_MODEL), lambda b, t: (0, 0)),
        ],
        out_specs=[
            pl.BlockSpec((1, D_MODEL, TM), colt),
            pl.BlockSpec((1, TM, 2 * KV_WIDTH), row),
            pl.BlockSpec((1, TM // LANES, 2 * KV_WIDTH, LANES), lambda b, t: (b, t, 0, 0)),
            pl.BlockSpec((1, N_KV_GROUPS * GATE_ROWS, TM), colt),
            pl.BlockSpec((1, TM, 2 * KV_WIDTH), row),
        ],
        out_shape=[
            jax.ShapeDtypeStruct((BATCH, D_MODEL, SEQ), BF16),
            jax.ShapeDtypeStruct((BATCH, SEQ, 2 * KV_WIDTH), BF16),
            jax.ShapeDtypeStruct((BATCH, SEQ // LANES, 2 * KV_WIDTH, LANES), BF16),
            jax.ShapeDtypeStruct((BATCH, N_KV_GROUPS * GATE_ROWS, SEQ), F32),
            jax.ShapeDtypeStruct((BATCH, SEQ, 2 * KV_WIDTH), F32),
        ],
        compiler_params=_cparams(2),
        name="nsa_proj",
    )(x, mod, wn, wt)


CHUNK_FEAT = CMP_STRIDE * HEAD_DIM


def _cmp_hidden(x_ref, posa_ref, posb_ref, w1a_ref, w1b_ref):
    x = x_ref[0, 0]
    xa = (x + posa_ref[0]).astype(BF16)
    xb = (x + posb_ref[0]).astype(BF16)
    a = jnp.dot(xa, w1a_ref[0], preferred_element_type=F32)
    b = jnp.dot(xb, w1b_ref[0], preferred_element_type=F32)
    return _gelu(a + pltpu.roll(b, N_CMP_PAD - 1, axis=0)).astype(BF16)


def _cmp_kernel(xk_ref, xv_ref, posa_ref, posb_ref, w1a_ref, w1b_ref, w2k_ref, w2vt_ref, posc_ref, kc_ref, vct_ref):
    hk = _cmp_hidden(xk_ref, posa_ref.at[0:1], posb_ref.at[0:1], w1a_ref.at[0:1], w1b_ref.at[0:1])
    kc_ref[0, 0] = (jnp.dot(hk, w2k_ref[...], preferred_element_type=F32) + posc_ref[...]).astype(BF16)
    hv = _cmp_hidden(xv_ref, posa_ref.at[1:2], posb_ref.at[1:2], w1a_ref.at[1:2], w1b_ref.at[1:2])
    vct = lax.dot_general(w2vt_ref[...], hv, (((1,), (1,)), ((), ())), preferred_element_type=F32)
    n = lax.broadcasted_iota(jnp.int32, vct.shape, 1)
    vct_ref[0, 0] = jnp.where(n < N_CMP, vct, 0.0).astype(BF16)


def _pos_features(pos_hi, pos_lo, n):
    feat = jnp.stack([pos_hi] * 3 + [pos_lo] * 3, axis=1).astype(F32)
    return jnp.pad(feat, ((0, 0), (AUG_POS0, AUG - AUG_POS0 - 6)))


def _compress(kvc, pos_k, pos_v, k_w1, k_w2, v_w1, v_w2):
    n_chunk = SEQ // CMP_STRIDE
    x = (kvc.reshape(BATCH, n_chunk, CMP_STRIDE, 2 * N_KV_GROUPS, HEAD_DIM)
         .transpose(0, 3, 1, 2, 4).reshape(BATCH, 2 * N_KV_GROUPS, n_chunk, CHUNK_FEAT))
    pos = jnp.stack([pos_k, pos_v])
    posa = pos[:, :CMP_STRIDE].reshape(2, 1, CHUNK_FEAT)
    posb = pos[:, CMP_STRIDE:].reshape(2, 1, CHUNK_FEAT)
    w1 = jnp.stack([k_w1, v_w1]).astype(BF16)
    w1a = w1[:, :CMP_STRIDE].reshape(2, CHUNK_FEAT, CMP_HIDDEN)
    w1b = w1[:, CMP_STRIDE:].reshape(2, CHUNK_FEAT, CMP_HIDDEN)
    w2k = jnp.pad(k_w2, ((0, 0), (0, AUG - HEAD_DIM))).astype(BF16)
    w2vt = v_w2.T.astype(BF16)
    nb = jnp.arange(N_CMP_PAD)
    posc = _pos_features(nb // 4, (nb % 4) * CMP_STRIDE + 0.5 * (CMP_BLOCK - 1), N_CMP_PAD)
    full = lambda shape: pl.BlockSpec(shape, lambda b, g: (0,) * len(shape))
    return pl.pallas_call(
        _cmp_kernel,
        grid=(BATCH, N_KV_GROUPS),
        in_specs=[
            pl.BlockSpec((1, 1, n_chunk, CHUNK_FEAT), lambda b, g: (b, g, 0, 0)),
            pl.BlockSpec((1, 1, n_chunk, CHUNK_FEAT), lambda b, g: (b, N_KV_GROUPS + g, 0, 0)),
            full((2, 1, CHUNK_FEAT)),
            full((2, 1, CHUNK_FEAT)),
            full((2, CHUNK_FEAT, CMP_HIDDEN)),
            full((2, CHUNK_FEAT, CMP_HIDDEN)),
            full((CMP_HIDDEN, AUG)),
            full((HEAD_DIM, CMP_HIDDEN)),
            full((N_CMP_PAD, AUG)),
        ],
        out_specs=[
            pl.BlockSpec((1, 1, N_CMP_PAD, AUG), lambda b, g: (b, g, 0, 0)),
            pl.BlockSpec((1, 1, HEAD_DIM, N_CMP_PAD), lambda b, g: (b, g, 0, 0)),
        ],
        out_shape=[
            jax.ShapeDtypeStruct((BATCH, N_KV_GROUPS, N_CMP_PAD, AUG), BF16),
            jax.ShapeDtypeStruct((BATCH, N_KV_GROUPS, HEAD_DIM, N_CMP_PAD), BF16),
        ],
        compiler_params=_cparams(2),
        name="nsa_compress",
    )(x, x, posa, posb, w1a, w1b, w2k, w2vt, posc)


def _select_top_n(score):
    n_sub = N_SEL_BLOCKS // SUBLANES
    pieces = [score[SUBLANES * v:SUBLANES * (v + 1), :] for v in range(n_sub)]
    ranks = [jnp.zeros((SUBLANES, TQ), F32) for _ in range(n_sub)]
    sub = lax.broadcasted_iota(jnp.int32, (SUBLANES, TQ), 0)
    for j2 in range(N_SEL_BLOCKS):
        v2, u2 = divmod(j2, SUBLANES)
        other = jnp.broadcast_to(score[j2:j2 + 1, :], (SUBLANES, TQ))
        for v in range(n_sub):
            if v < v2:
                beats = other > pieces[v]
            elif v > v2:
                beats = other >= pieces[v]
            else:
                beats = (other > pieces[v]) | ((other == pieces[v]) & (sub > u2))
            ranks[v] = ranks[v] + jnp.where(beats, 1.0, 0.0)
    return jnp.concatenate(ranks, axis=0) < N_SELECT


def _attn_kernel(qt_ref, kk_ref, vt_ref, kc_ref, vct_ref, gt_ref, rs_ref, rw_ref, consts_ref, constw_ref,
                 ovlt_ref, fixed_ref, cmpd_ref, diagd_ref, wmask_ref,
                 o_ref, ksa_ref, kwa_ref, vst_ref, vwt_ref, wq_ref, m_ref, acc_ref, tile_flag_ref):
    qi = pl.program_id(2)
    q0 = qi * TQ2
    n_vt = SEQ // LANES
    n_wpad = WINDOW // LANES

    @pl.when(qi == 0)
    def _():
        rows = 512

        def fill(i, carry):
            r0 = pl.multiple_of(i * rows, rows)
            kk = kk_ref[0, pl.ds(r0, rows), :]
            ksa_ref[pl.ds(r0, rows), :] = (jnp.dot(kk, rs_ref[...], preferred_element_type=F32)
                                           + consts_ref[pl.ds(r0, rows), :].astype(F32)).astype(BF16)
            kwa_ref[pl.ds(WINDOW + r0, rows), :] = (jnp.dot(kk, rw_ref[...], preferred_element_type=F32)
                                                    + constw_ref[pl.ds(r0, rows), :].astype(F32)).astype(BF16)
            return carry

        lax.fori_loop(0, SEQ // rows, fill, 0)
        kwa_ref[0:WINDOW, :] = (lax.broadcasted_iota(jnp.int32, (WINDOW, AUG), 1) == AUG_PAD).astype(BF16)
        ones_pad = (lax.broadcasted_iota(jnp.int32, (n_vt, V_ROWS - HEAD_DIM, LANES), 1) == 0).astype(BF16)
        vst_ref[:, 0:HEAD_DIM, :] = vt_ref[0, :, 0:HEAD_DIM, :]
        vst_ref[:, HEAD_DIM:, :] = ones_pad
        vwt_ref[0:n_wpad] = jnp.zeros((n_wpad, V_ROWS, LANES), BF16)
        vwt_ref[n_wpad:, 0:HEAD_DIM, :] = vt_ref[0, :, HEAD_DIM:, :]
        vwt_ref[n_wpad:, HEAD_DIM:, :] = ones_pad
        wq_ref[AUG_SEL0:AUG_SEL0 + N_SEL_BLOCKS, :] = jnp.zeros((N_SEL_BLOCKS, N_CHAIN * HQ), BF16)
        wq_ref[AUG_POS0:AUG_POS0 + AUG_FIXED_ROWS, :] = fixed_ref[0]
        wq_ref[AUG_POS0 + AUG_FIXED_ROWS:, :] = jnp.zeros((AUG - AUG_POS0 - AUG_FIXED_ROWS, N_CHAIN * HQ), BF16)

    qt = qt_ref[0]
    for c in range(N_CHAIN):
        for r in range(HEADS_PER_GROUP):
            lane0 = c * HQ + r * TQ
            wq_ref[0:HEAD_DIM, lane0:lane0 + TQ] = qt[r * HEAD_DIM:(r + 1) * HEAD_DIM, c * TQ:(c + 1) * TQ]

    def wq_chain(c):
        return wq_ref[:, c * HQ:(c + 1) * HQ]

    lane_q = lax.broadcasted_iota(jnp.int32, (1, HQ), 1) & (TQ - 1)

    def run_jobs(jobs, ahead=2):
        scores = [None] * len(jobs)
        for i in range(min(ahead, len(jobs))):
            scores[i] = jobs[i][0]()
        for i, (_, soft, value) in enumerate(jobs):
            mid = soft(scores[i])
            if i + ahead < len(jobs):
                scores[i + ahead] = jobs[i + ahead][0]()
            value(mid)

    def v_tiles(v_ref, row0, tk):
        vt0 = row0 >> _log2(LANES)
        return jnp.concatenate([v_ref[vt0 + j] for j in range(tk // LANES)], axis=1)

    oc_t = [None] * N_CHAIN
    imp_t = [None] * N_CHAIN

    def cmp_job(c):
        q0c = q0 + c * TQ

        def score():
            return jnp.dot(kc_ref[0, 0], wq_chain(c), preferred_element_type=F32)

        def soft(s):
            s = jnp.where(cmpd_ref[...] <= q0c, s, NEG_INF)
            p = jnp.exp2(s - jnp.max(s, axis=0, keepdims=True))
            has_c = (q0c + lane_q >= CMP_BLOCK - 1).astype(F32)
            p = p * (has_c / jnp.sum(p, axis=0, keepdims=True))
            psum = p[:, 0:TQ] + p[:, TQ:2 * TQ] + p[:, 2 * TQ:3 * TQ] + p[:, 3 * TQ:4 * TQ]
            return p.astype(BF16), _split3(psum)

        def value(mid):
            pb, (p1, p2, p3) = mid
            oc_t[c] = jnp.dot(vct_ref[0, 0], pb, preferred_element_type=F32)
            ovlt = ovlt_ref[...]
            imp_t[c] = (jnp.dot(ovlt, p1, preferred_element_type=F32)
                        + jnp.dot(ovlt, p2, preferred_element_type=F32)
                        + jnp.dot(ovlt, p3, preferred_element_type=F32))

        return score, soft, value

    ow_t = [None] * N_CHAIN

    def win_job(c):
        row0 = pl.multiple_of(q0 + c * TQ, LANES)

        def score():
            return jnp.dot(kwa_ref[pl.ds(row0, TKW), :], wq_chain(c), preferred_element_type=F32)

        def soft(sc):
            sc = sc + wmask_ref[...]
            return jnp.exp2(sc - jnp.max(sc, axis=0, keepdims=True)).astype(BF16)

        def value(pt):
            acc = jnp.dot(v_tiles(vwt_ref, row0, TKW), pt, preferred_element_type=F32)
            ow_t[c] = acc[0:HEAD_DIM] / acc[HEAD_DIM:HEAD_DIM + 1]

        return score, soft, value

    run_jobs([cmp_job(c) for c in range(N_CHAIN)] + [win_job(c) for c in range(N_CHAIN)])

    sel_bias = []
    unselected = None
    for c in range(N_CHAIN):
        q0c = q0 + c * TQ
        jb = lax.broadcasted_iota(jnp.int32, (N_SEL_BLOCKS, TQ), 0)
        bt = (q0c + lax.broadcasted_iota(jnp.int32, (N_SEL_BLOCKS, TQ), 1)) >> _log2(SEL_BLOCK)
        forced = (jb == 0) | (jb == bt) | (jb == bt - 1)
        score = jnp.where(forced, FORCE_SCORE, jnp.where(jb <= bt, imp_t[c], -FORCE_SCORE))
        bias = jnp.where(_select_top_n(score), 0.0, NEG_INF)
        unselected = bias if unselected is None else jnp.maximum(unselected, bias)
        sel_bias.append(bias.astype(BF16))
    for t in range(SEQ // TKS):
        tile_bias = unselected[t * (TKS // SEL_BLOCK):(t + 1) * (TKS // SEL_BLOCK), :]
        tile_flag_ref[t] = jnp.max(tile_bias)
    for c in range(N_CHAIN):
        for r in range(HEADS_PER_GROUP):
            lane0 = c * HQ + r * TQ
            wq_ref[AUG_SEL0:AUG_SEL0 + N_SEL_BLOCKS, lane0:lane0 + TQ] = sel_bias[c]

    m_ref[...] = jnp.full_like(m_ref, NEG_INF)
    acc_ref[...] = jnp.zeros_like(acc_ref)

    def sel_job(c, row0, tk, diagonal):
        row0 = pl.multiple_of(row0, LANES)

        def score():
            return jnp.dot(ksa_ref[pl.ds(row0, tk), :], wq_chain(c), preferred_element_type=F32)

        def soft(sc):
            if diagonal:
                sc = jnp.where(diagd_ref[0:tk, :] <= TQ * c, sc, NEG_INF)
            m_old = m_ref[c]
            m_new = jnp.maximum(m_old, jnp.max(sc, axis=0, keepdims=True))
            m_ref[c] = m_new
            return jnp.exp2(sc - m_new).astype(BF16), jnp.exp2(m_old - m_new)

        def value(mid):
            pt, alpha = mid
            acc_ref[c] = alpha * acc_ref[c] + jnp.dot(v_tiles(vst_ref, row0, tk), pt, preferred_element_type=F32)

        return score, soft, value

    n_full = q0 >> _log2(TKS)

    def tile_jobs(tile):
        return [sel_job(c, tile * TKS + s * TKJ, TKJ, False) for s in range(TKS // TKJ) for c in range(N_CHAIN)]

    def tile_selected(tile):
        return tile_flag_ref[tile] == 0.0

    def sel_pair(kt2, carry):
        @pl.when(tile_selected(2 * kt2) | tile_selected(2 * kt2 + 1))
        def _():
            run_jobs(tile_jobs(2 * kt2) + tile_jobs(2 * kt2 + 1), ahead=4)
        return carry

    lax.fori_loop(0, n_full >> 1, sel_pair, 0)

    def end_jobs(with_odd):
        jobs = tile_jobs(n_full - 1) if with_odd else []
        return jobs + [sel_job(c, q0, TQ * (c + 1), True) for c in range(N_CHAIN)]

    with_odd_tile = ((n_full & 1) != 0) & tile_selected(jnp.maximum(n_full - 1, 0))
    for with_odd in (False, True):
        @pl.when(with_odd_tile == with_odd)
        def _():
            run_jobs(end_jobs(with_odd), ahead=3)

    os_t = []
    for c in range(N_CHAIN):
        acc = acc_ref[c]
        os_t.append(acc[0:HEAD_DIM] / acc[HEAD_DIM:HEAD_DIM + 1])

    gt = gt_ref[0]
    cols = []
    for c in range(N_CHAIN):
        blocks = []
        for r in range(HEADS_PER_GROUP):
            sl = slice(r * TQ, (r + 1) * TQ)
            g3 = gt[3 * r:3 * r + 3, c * TQ:(c + 1) * TQ]
            blocks.append(g3[0:1] * oc_t[c][:, sl] + g3[1:2] * os_t[c][:, sl] + g3[2:3] * ow_t[c][:, sl])
        cols.append(jnp.concatenate(blocks, axis=0))
    o_ref[0] = jnp.concatenate(cols, axis=1).T.astype(BF16)


def _attention(qt, kk, vt, kc, vct, gt):
    hw = HEADS_PER_GROUP * HEAD_DIM
    src = jnp.arange(LANES)
    col = jnp.arange(AUG)
    rs = (src[:, None] == col[None, :]).astype(BF16) * (src[:, None] < HEAD_DIM)
    rw = (src[:, None] - HEAD_DIM == col[None, :]).astype(BF16) * (src[:, None] >= HEAD_DIM)
    key = jnp.arange(SEQ)
    pos = _pos_features(key // SEL_BLOCK, key % SEL_BLOCK, SEQ)
    onehot = (col[None, :] - AUG_SEL0 == (key // SEL_BLOCK)[:, None]).astype(F32)
    consts = (pos + onehot).astype(BF16)
    constw = pos.astype(BF16)
    cmp_start = jnp.arange(N_CMP_PAD) * CMP_STRIDE
    sel_start = jnp.arange(N_SEL_BLOCKS) * SEL_BLOCK
    ovlt = ((cmp_start[None, :] <= sel_start[:, None] + SEL_BLOCK - 1)
            & (cmp_start[None, :] + CMP_BLOCK - 1 >= sel_start[:, None])
            & (jnp.arange(N_CMP_PAD)[None, :] < N_CMP)).astype(BF16)
    hh = jnp.arange(1, N_HEADS + 1, dtype=F32)
    slope = jnp.exp2(-8.0 * hh / N_HEADS) * LOG2E
    s1, s2, s3 = _split3(slope)
    parts = jnp.stack([s1, s2, s3]).astype(F32)
    rows = jnp.concatenate([parts * SEL_BLOCK, parts,
                            jnp.zeros((AUG_POS_ROWS - 6, N_HEADS), F32)])
    slt = jnp.repeat(rows.reshape(AUG_POS_ROWS, N_KV_GROUPS, HEADS_PER_GROUP).transpose(1, 0, 2),
                     TQ, axis=2)
    pad_rows = jnp.zeros((N_KV_GROUPS, AUG_POS_ROWS, HQ), F32).at[:, 0, :].set(NEG_INF)
    fixed = jnp.tile(jnp.concatenate([slt, pad_rows], axis=1), (1, 1, N_CHAIN)).astype(BF16)
    lane_i = jnp.arange(HQ) % TQ
    nb = jnp.arange(N_CMP_PAD)
    cmpd = jnp.where(nb[:, None] < N_CMP, nb[:, None] * CMP_STRIDE + (CMP_BLOCK - 1) - lane_i[None, :],
                     2 * SEQ).astype(jnp.int32)

    def key_mask(n_keys, lo, hi):
        d = jnp.arange(n_keys)[:, None] - lane_i[None, :]
        return jnp.where((d >= lo) & (d <= hi), 0.0, NEG_INF).astype(F32)

    diagd = (jnp.arange(TQ2)[:, None] - lane_i[None, :]).astype(jnp.int32)
    wmask = key_mask(TKW, 1, WINDOW)

    nq = SEQ // TQ2
    n_vt = SEQ // LANES
    const = lambda shape: pl.BlockSpec(shape, lambda b, g, i: (0,) * len(shape))
    return pl.pallas_call(
        _attn_kernel,
        grid=(BATCH, N_KV_GROUPS, nq),
        in_specs=[
            pl.BlockSpec((1, hw, TQ2), lambda b, g, i: (b, g, i)),
            pl.BlockSpec((1, SEQ, LANES), lambda b, g, i: (b, 0, g)),
            pl.BlockSpec((1, n_vt, LANES, LANES), lambda b, g, i: (b, 0, g, 0)),
            pl.BlockSpec((1, 1, N_CMP_PAD, AUG), lambda b, g, i: (b, g, 0, 0)),
            pl.BlockSpec((1, 1, HEAD_DIM, N_CMP_PAD), lambda b, g, i: (b, g, 0, 0)),
            pl.BlockSpec((1, GATE_ROWS, TQ2), lambda b, g, i: (b, g, i)),
            const((LANES, AUG)),
            const((LANES, AUG)),
            const((SEQ, AUG)),
            const((SEQ, AUG)),
            const((N_SEL_BLOCKS, N_CMP_PAD)),
            pl.BlockSpec((1, AUG_FIXED_ROWS, N_CHAIN * HQ), lambda b, g, i: (g, 0, 0)),
            const((N_CMP_PAD, HQ)),
            const((TQ2, HQ)),
            const((TKW, HQ)),
        ],
        out_specs=pl.BlockSpec((1, TQ2, hw), lambda b, g, i: (b, i, g)),
        out_shape=jax.ShapeDtypeStruct((BATCH, SEQ, D_MODEL), BF16),
        scratch_shapes=[
            pltpu.VMEM((SEQ, AUG), BF16),
            pltpu.VMEM((WINDOW + SEQ, AUG), BF16),
            pltpu.VMEM((n_vt, V_ROWS, LANES), BF16),
            pltpu.VMEM((WINDOW // LANES + n_vt, V_ROWS, LANES), BF16),
            pltpu.VMEM((AUG, N_CHAIN * HQ), BF16),
            pltpu.VMEM((N_CHAIN, 1, HQ), F32),
            pltpu.VMEM((N_CHAIN, V_ROWS, HQ), F32),
            pltpu.SMEM((SEQ // TKS,), F32),
        ],
        compiler_params=_cparams(3),
        name="nsa_attention",
    )(qt, kk, vt, kc, vct, gt, rs, rw, consts, constw, ovlt, fixed, cmpd, diagd, wmask)


def _out_kernel(o_ref, x_ref, mod_ref, w_ref, lng_ref, lnb_ref, y_ref):
    y = jnp.dot(o_ref[0], w_ref[...], preferred_element_type=F32)
    gate = mod_ref[0][2:3]
    y_ref[0] = _deepnorm_ln(x_ref[0], gate, y, lng_ref[...], lnb_ref[...])


def _out_proj(o, x, mod, w_out, ln_g, ln_b):
    row = lambda b, t: (b, t, 0)
    c2 = lambda b, t: (0, 0)
    return pl.pallas_call(
        _out_kernel,
        grid=(BATCH, SEQ // TM),
        in_specs=[
            pl.BlockSpec((1, TM, D_MODEL), row),
            pl.BlockSpec((1, TM, D_MODEL), row),
            pl.BlockSpec((1, 6, D_MODEL), lambda b, t: (b, 0, 0)),
            pl.BlockSpec((D_MODEL, D_MODEL), c2),
            pl.BlockSpec((1, D_MODEL), c2),
            pl.BlockSpec((1, D_MODEL), c2),
        ],
        out_specs=pl.BlockSpec((1, TM, D_MODEL), row),
        out_shape=jax.ShapeDtypeStruct((BATCH, SEQ, D_MODEL), F32),
        compiler_params=_cparams(2),
        name="nsa_out_proj",
    )(o, x, mod, w_out.astype(BF16), ln_g.reshape(1, D_MODEL), ln_b.reshape(1, D_MODEL))


def _nsa_layer(x, mod, w_in, pos_k, pos_v, k_w1, k_w2, v_w1, v_w2, w_out, ln_g, ln_b):
    qt, kk, vt, gt, kvc = _nsa_proj(x, mod, w_in)
    kc, vct = _compress(kvc, pos_k, pos_v, k_w1, k_w2, v_w1, v_w2)
    o = _attention(qt, kk, vt, kc, vct, gt)
    return _out_proj(o, x, mod, w_out, ln_g, ln_b)


def kernel(x, c, ada_w, ada_b, ln1_g, ln1_b, ln2_g, ln2_b, ffn_w_in, ffn_conv, ffn_w_out, conv_w_in, conv_w,
           conv_w_out, pool_w_in, pool_w_grp, pool_scale, pool_w_out, nsa_w_in, nsa_cmp_pos_k, nsa_cmp_pos_v,
           nsa_cmp_k_w1, nsa_cmp_k_w2, nsa_cmp_v_w1, nsa_cmp_v_w2, nsa_w_out):
    assert x.shape == (BATCH, SEQ, D_MODEL) and x.dtype == F32
    mods = _modulation(c, ada_w, ada_b).reshape(DEPTH, BATCH, 6, D_MODEL)
    for i in range(DEPTH):
        mod = mods[i]
        m, j = i % N_MIXERS, i // N_MIXERS
        if m == 0:
            x = _gated_layer("conv", x, mod, 0, conv_w_in[j], conv_w[j], conv_w_out[j], ln1_g[i], ln1_b[i])
        elif m == 1:
            x = _pool_layer(x, mod, pool_w_in[j], pool_w_grp[j], pool_scale[j], pool_w_out[j], ln1_g[i], ln1_b[i])
        else:
            x = _nsa_layer(x, mod, nsa_w_in[j], nsa_cmp_pos_k[j], nsa_cmp_pos_v[j], nsa_cmp_k_w1[j],
                           nsa_cmp_k_w2[j], nsa_cmp_v_w1[j], nsa_cmp_v_w2[j], nsa_w_out[j], ln1_g[i], ln1_b[i])
        x = _gated_layer("ffn", x, mod, 3, ffn_w_in[i], ffn_conv[i], ffn_w_out[i], ln2_g[i], ln2_b[i])
    return x
```

```python
import functools
import math

import jax
import jax.numpy as jnp
from jax import lax
from jax.experimental import pallas as pl
from jax.experimental.pallas import tpu as pltpu

F32 = jnp.float32
BF16 = jnp.bfloat16

D_MODEL = 1024
BATCH = 16
SEQ = 4096
DEPTH = 4
N_MIXERS = 3
CONV_WIDTH = 3
D_FF = 2816
POOL_WINDOWS = (2, 4, 8, 16)
N_POOL_GROUPS = 4
POOL_GROUP = D_MODEL // N_POOL_GROUPS
N_HEADS = 16
HEAD_DIM = 64
N_KV_GROUPS = 4
HEADS_PER_GROUP = 4
KV_WIDTH = N_KV_GROUPS * HEAD_DIM
CMP_BLOCK = 32
CMP_STRIDE = 16
CMP_HIDDEN = 256
SEL_BLOCK = 64
N_SELECT = 16
WINDOW = 512
N_BRANCH = 3
ALPHA = (2.0 * DEPTH) ** 0.25
LN_EPS = 1e-5
NEG_INF = -1e30
FORCE_SCORE = 1e4
LOG2E = math.log2(math.e)

N_CMP = (SEQ - CMP_BLOCK) // CMP_STRIDE + 1
N_CMP_PAD = 256
N_SEL_BLOCKS = SEQ // SEL_BLOCK
LANES = 128
SUBLANES = 8

TM = 512
FC = 256
CONV_HALO = 8
POOL_HALO = 16
TQ = 128
N_CHAIN = 4
TQ2 = N_CHAIN * TQ
HQ = HEADS_PER_GROUP * TQ
TKS = 512
TKJ = 256
TKW = WINDOW + TQ
VMEM_LIMIT = 56 * 1024 * 1024

AUG = 256
AUG_SEL0 = HEAD_DIM
AUG_POS0 = 2 * HEAD_DIM
AUG_POS_ROWS = 16
AUG_PAD = AUG_POS0 + AUG_POS_ROWS
AUG_FIXED_ROWS = 2 * AUG_POS_ROWS
V_ROWS = 80
GATE_ROWS = 16


def _cparams(n_axes):
    return pltpu.CompilerParams(dimension_semantics=("arbitrary",) * n_axes,
                                vmem_limit_bytes=VMEM_LIMIT)


def _log2(n):
    assert n & (n - 1) == 0
    return n.bit_length() - 1


def _gelu(x):
    c = math.sqrt(2.0 / math.pi)
    return x * (0.5 * (1.0 + jnp.tanh(c * (x + 0.044715 * (x * x * x)))))


def _deepnorm_ln(x, gate, y, ln_g, ln_b):
    r = ALPHA * x + (1.0 + gate) * y
    mu = jnp.mean(r, axis=-1, keepdims=True)
    d = r - mu
    var = jnp.mean(d * d, axis=-1, keepdims=True)
    return d * lax.rsqrt(var + LN_EPS) * ln_g + ln_b


def _split3(a):
    a1 = a.astype(BF16)
    r1 = a - a1.astype(F32)
    a2 = r1.astype(BF16)
    a3 = (r1 - a2.astype(F32)).astype(BF16)
    return a1, a2, a3


def _ada_kernel(c_ref, w_ref, b_ref, o_ref):
    c = c_ref[...]
    cond = c * jax.nn.sigmoid(c)
    c1, c2, c3 = _split3(cond)
    w1, w2, w3 = _split3(w_ref[0])
    acc = jnp.dot(c1, w1, preferred_element_type=F32)
    acc += jnp.dot(c1, w2, preferred_element_type=F32)
    acc += jnp.dot(c2, w1, preferred_element_type=F32)
    acc += jnp.dot(c1, w3, preferred_element_type=F32)
    acc += jnp.dot(c2, w2, preferred_element_type=F32)
    acc += jnp.dot(c3, w1, preferred_element_type=F32)
    o_ref[0] = acc + b_ref[0]


def _modulation(c, ada_w, ada_b):
    nc = 1536
    n_col = 6 * D_MODEL // nc
    return pl.pallas_call(
        _ada_kernel,
        grid=(DEPTH, n_col),
        in_specs=[
            pl.BlockSpec((BATCH, D_MODEL), lambda i, j: (0, 0)),
            pl.BlockSpec((1, D_MODEL, nc), lambda i, j: (i, 0, j)),
            pl.BlockSpec((1, 1, nc), lambda i, j: (i, 0, j)),
        ],
        out_specs=pl.BlockSpec((1, BATCH, nc), lambda i, j: (i, 0, j)),
        out_shape=jax.ShapeDtypeStruct((DEPTH, BATCH, 6 * D_MODEL), F32),
        compiler_params=_cparams(2),
        name="modulation",
    )(c, ada_w, ada_b.reshape(DEPTH, 1, 6 * D_MODEL))


def _gated_kernel(mode, n_chunks, mrow, x_ref, mod_ref, win_ref, wconv_ref, wout_ref, lng_ref, lnb_ref,
                  o_ref, h_ref, carry_ref, cbuf_ref, acc_ref):
    t = pl.program_id(1)

    @pl.when(t == 0)
    def _():
        carry_ref[...] = jnp.zeros_like(carry_ref)

    m = mod_ref[0]
    sh, sc, gate = m[mrow:mrow + 1], m[mrow + 1:mrow + 2], m[mrow + 2:mrow + 3]
    feat = n_chunks * FC
    n_parts = 3 if mode == "conv" else 2

    xt = x_ref[0]
    h_ref[...] = (xt * (1.0 + sc) + sh).astype(BF16)

    def in_proj(j):
        h = h_ref[...]
        return [jnp.dot(h, win_ref[:, p * feat + j * FC:p * feat + (j + 1) * FC], preferred_element_type=F32)
                for p in range(n_parts)]

    p_next = in_proj(0)
    for j in range(n_chunks):
        p = p_next
        if mode == "conv":
            gate_in = p[0]
            cin = p[1] * p[2]
        else:
            cin, gate_in = p
        cbuf_ref[0:CONV_HALO, :] = carry_ref[j]
        cbuf_ref[CONV_HALO:, :] = cin
        carry_ref[j] = cin[TM - CONV_HALO:, :]
        w = wconv_ref[j]
        y = (w[0:1] * cbuf_ref[CONV_HALO - 2:CONV_HALO - 2 + TM, :]
             + w[1:2] * cbuf_ref[CONV_HALO - 1:CONV_HALO - 1 + TM, :]
             + w[2:3] * cin)
        if mode == "conv":
            z = gate_in * y
        else:
            z = _gelu(y) * gate_in
        z = z.astype(BF16)
        if j + 1 < n_chunks:
            p_next = in_proj(j + 1)
        out = jnp.dot(z, wout_ref[j], preferred_element_type=F32)
        if j == 0:
            acc_ref[...] = out
        else:
            acc_ref[...] += out

    o_ref[0] = _deepnorm_ln(xt, gate, acc_ref[...], lng_ref[...], lnb_ref[...])


def _gated_layer(mode, x, mod, mrow, w_in, w_conv, w_out, ln_g, ln_b):
    feat = w_out.shape[0]
    n_parts = w_in.shape[1] // feat
    n_chunks = feat // FC
    assert n_parts == (3 if mode == "conv" else 2)
    wout_r = w_out.astype(BF16).reshape(n_chunks, FC, D_MODEL)
    wconv_r = w_conv.reshape(CONV_WIDTH, n_chunks, FC).transpose(1, 0, 2)
    const3 = lambda b, t: (0, 0, 0)
    return pl.pallas_call(
        functools.partial(_gated_kernel, mode, n_chunks, mrow),
        grid=(BATCH, SEQ // TM),
        in_specs=[
            pl.BlockSpec((1, TM, D_MODEL), lambda b, t: (b, t, 0)),
            pl.BlockSpec((1, 6, D_MODEL), lambda b, t: (b, 0, 0)),
            pl.BlockSpec((D_MODEL, n_parts * feat), lambda b, t: (0, 0)),
            pl.BlockSpec((n_chunks, CONV_WIDTH, FC), const3),
            pl.BlockSpec((n_chunks, FC, D_MODEL), const3),
            pl.BlockSpec((1, D_MODEL), lambda b, t: (0, 0)),
            pl.BlockSpec((1, D_MODEL), lambda b, t: (0, 0)),
        ],
        out_specs=pl.BlockSpec((1, TM, D_MODEL), lambda b, t: (b, t, 0)),
        out_shape=jax.ShapeDtypeStruct((BATCH, SEQ, D_MODEL), F32),
        scratch_shapes=[
            pltpu.VMEM((TM, D_MODEL), BF16),
            pltpu.VMEM((n_chunks, CONV_HALO, FC), F32),
            pltpu.VMEM((CONV_HALO + TM, FC), F32),
            pltpu.VMEM((TM, D_MODEL), F32),
        ],
        compiler_params=_cparams(2),
        name="gated_" + mode,
    )(x, mod, w_in.astype(BF16), wconv_r, wout_r, ln_g.reshape(1, D_MODEL), ln_b.reshape(1, D_MODEL))


def _pool_kernel(x_ref, mod_ref, win_ref, wgrp_ref, scale_ref, wout_ref, lng_ref, lnb_ref,
                 o_ref, carry_ref, ubuf_ref, z_ref):
    t = pl.program_id(1)

    @pl.when(t == 0)
    def _():
        carry_ref[...] = jnp.zeros_like(carry_ref)

    xt = x_ref[0]
    m = mod_ref[0]
    sh, sc, gate = m[0:1], m[1:2], m[2:3]
    h = (xt * (1.0 + sc) + sh).astype(BF16)
    u = jnp.dot(h, win_ref[...], preferred_element_type=F32)
    ubuf_ref[0:POOL_HALO, :] = carry_ref[...]
    ubuf_ref[POOL_HALO:, :] = u
    carry_ref[...] = u[TM - POOL_HALO:, :]
    pos = t * TM + lax.broadcasted_iota(jnp.int32, (TM, 1), 0)
    for g, w in enumerate(POOL_WINDOWS):
        lo, hi = g * POOL_GROUP, (g + 1) * POOL_GROUP
        e = ubuf_ref[:, lo:hi]
        s = e
        k = 1
        while k < w:
            s = s + pltpu.roll(s, k, axis=0)
            k *= 2
        cnt = jnp.minimum(pos + 1, w).astype(F32)
        pooled = s[POOL_HALO:, :] / cnt - e[POOL_HALO:, :]
        zg = jnp.dot(pooled.astype(BF16), wgrp_ref[g], preferred_element_type=F32)
        z_ref[:, lo:hi] = (zg * scale_ref[:, lo:hi]).astype(BF16)
    y = jnp.dot(z_ref[...], wout_ref[...], preferred_element_type=F32)
    o_ref[0] = _deepnorm_ln(xt, gate, y, lng_ref[...], lnb_ref[...])


def _pool_layer(x, mod, w_in, w_grp, scale, w_out, ln_g, ln_b):
    c2 = lambda b, t: (0, 0)
    return pl.pallas_call(
        _pool_kernel,
        grid=(BATCH, SEQ // TM),
        in_specs=[
            pl.BlockSpec((1, TM, D_MODEL), lambda b, t: (b, t, 0)),
            pl.BlockSpec((1, 6, D_MODEL), lambda b, t: (b, 0, 0)),
            pl.BlockSpec((D_MODEL, D_MODEL), c2),
            pl.BlockSpec((N_POOL_GROUPS, POOL_GROUP, POOL_GROUP), lambda b, t: (0, 0, 0)),
            pl.BlockSpec((1, D_MODEL), c2),
            pl.BlockSpec((D_MODEL, D_MODEL), c2),
            pl.BlockSpec((1, D_MODEL), c2),
            pl.BlockSpec((1, D_MODEL), c2),
        ],
        out_specs=pl.BlockSpec((1, TM, D_MODEL), lambda b, t: (b, t, 0)),
        out_shape=jax.ShapeDtypeStruct((BATCH, SEQ, D_MODEL), F32),
        scratch_shapes=[
            pltpu.VMEM((POOL_HALO, D_MODEL), F32),
            pltpu.VMEM((POOL_HALO + TM, D_MODEL), F32),
            pltpu.VMEM((TM, D_MODEL), BF16),
        ],
        compiler_params=_cparams(2),
        name="pool_mixer",
    )(x, mod, w_in.astype(BF16), w_grp.astype(BF16), scale.reshape(1, D_MODEL), w_out.astype(BF16),
      ln_g.reshape(1, D_MODEL), ln_b.reshape(1, D_MODEL))


NSA_T_ROWS = D_MODEL + 2 * KV_WIDTH + N_KV_GROUPS * GATE_ROWS
Q_SCALE = HEAD_DIM ** -0.5 * LOG2E


def _nsa_proj_kernel(x_ref, mod_ref, wn_ref, wt_ref, qt_ref, kk_ref, vt_ref, gt_ref, kvc_ref):
    xt = x_ref[0]
    m = mod_ref[0]
    sh, sc = m[0:1], m[1:2]
    h = (xt * (1.0 + sc) + sh).astype(BF16)
    rn = jnp.dot(h, wn_ref[...], preferred_element_type=F32)
    kk_ref[0] = rn[:, :2 * KV_WIDTH].astype(BF16)
    kvc_ref[0] = rn[:, 2 * KV_WIDTH:]
    rt = lax.dot_general(wt_ref[...], h, (((1,), (1,)), ((), ())), preferred_element_type=F32)
    qt_ref[0] = (rt[0:D_MODEL] * Q_SCALE).astype(BF16)
    v_t = rt[D_MODEL:D_MODEL + 2 * KV_WIDTH].astype(BF16)
    for j in range(TM // LANES):
        vt_ref[0, j] = v_t[:, j * LANES:(j + 1) * LANES]
    gt_ref[0] = jax.nn.sigmoid(rt[D_MODEL + 2 * KV_WIDTH:])


def _nsa_proj(x, mod, w_in):
    kc0 = D_MODEL
    vc0, ks0, vs0, kw0, vw0, gl0 = (kc0 + i * KV_WIDTH for i in range(1, 7))
    grp = lambda c0, g: jnp.arange(c0 + g * HEAD_DIM, c0 + (g + 1) * HEAD_DIM)
    cols_n = jnp.concatenate([jnp.concatenate([grp(ks0, g), grp(kw0, g)]) for g in range(N_KV_GROUPS)]
                             + [jnp.arange(kc0, ks0)])
    wn = jnp.take(w_in, cols_n, axis=1).astype(BF16)
    cols_v = jnp.concatenate([jnp.concatenate([grp(vs0, g), grp(vw0, g)]) for g in range(N_KV_GROUPS)])
    n_gate = HEADS_PER_GROUP * N_BRANCH
    w_gate = w_in[:, gl0:gl0 + N_KV_GROUPS * n_gate].reshape(D_MODEL, N_KV_GROUPS, n_gate)
    w_gate = jnp.pad(w_gate, ((0, 0), (0, 0), (0, GATE_ROWS - n_gate))).reshape(D_MODEL, N_KV_GROUPS * GATE_ROWS)
    wt = jnp.concatenate([w_in[:, :D_MODEL], jnp.take(w_in, cols_v, axis=1), w_gate], axis=1).T.astype(BF16)
    row = lambda b, t: (b, t, 0)
    colt = lambda b, t: (b, 0, t)
    return pl.pallas_call(
        _nsa_proj_kernel,
        grid=(BATCH, SEQ // TM),
        in_specs=[
            pl.BlockSpec((1, TM, D_MODEL), row),
            pl.BlockSpec((1, 6, D_MODEL), lambda b, t: (b, 0, 0)),
            pl.BlockSpec((D_MODEL, 4 * KV_WIDTH), lambda b, t: (0, 0)),
            pl.BlockSpec((NSA_T_ROWS, D_MODEL), lambda b, t: (0, 0)),
        ],
        out_specs=[
            pl.BlockSpec((1, D_MODEL, TM), colt),
            pl.BlockSpec((1, TM, 2 * KV_WIDTH), row),
            pl.BlockSpec((1, TM // LANES, 2 * KV_WIDTH, LANES), lambda b, t: (b, t, 0, 0)),
            pl.BlockSpec((1, N_KV_GROUPS * GATE_ROWS, TM), colt),
            pl.BlockSpec((1, TM, 2 * KV_WIDTH), row),
        ],
        out_shape=[
            jax.ShapeDtypeStruct((BATCH, D_MODEL, SEQ), BF16),
            jax.ShapeDtypeStruct((BATCH, SEQ, 2 * KV_WIDTH), BF16),
            jax.ShapeDtypeStruct((BATCH, SEQ // LANES, 2 * KV_WIDTH, LANES), BF16),
            jax.ShapeDtypeStruct((BATCH, N_KV_GROUPS * GATE_ROWS, SEQ), F32),
            jax.ShapeDtypeStruct((BATCH, SEQ, 2 * KV_WIDTH), F32),
        ],
        compiler_params=_cparams(2),
        name="nsa_proj",
    )(x, mod, wn, wt)


CHUNK_FEAT = CMP_STRIDE * HEAD_DIM


def _cmp_hidden(x_ref, posa_ref, posb_ref, w1a_ref, w1b_ref):
    x = x_ref[0, 0]
    xa = (x + posa_ref[0]).astype(BF16)
    xb = (x + posb_ref[0]).astype(BF16)
    a = jnp.dot(xa, w1a_ref[0], preferred_element_type=F32)
    b = jnp.dot(xb, w1b_ref[0], preferred_element_type=F32)
    return _gelu(a + pltpu.roll(b, N_CMP_PAD - 1, axis=0)).astype(BF16)


def _cmp_kernel(xk_ref, xv_ref, posa_ref, posb_ref, w1a_ref, w1b_ref, w2k_ref, w2vt_ref, posc_ref, kc_ref, vct_ref):
    hk = _cmp_hidden(xk_ref, posa_ref.at[0:1], posb_ref.at[0:1], w1a_ref.at[0:1], w1b_ref.at[0:1])
    kc_ref[0, 0] = (jnp.dot(hk, w2k_ref[...], preferred_element_type=F32) + posc_ref[...]).astype(BF16)
    hv = _cmp_hidden(xv_ref, posa_ref.at[1:2], posb_ref.at[1:2], w1a_ref.at[1:2], w1b_ref.at[1:2])
    vct = lax.dot_general(w2vt_ref[...], hv, (((1,), (1,)), ((), ())), preferred_element_type=F32)
    n = lax.broadcasted_iota(jnp.int32, vct.shape, 1)
    vct_ref[0, 0] = jnp.where(n < N_CMP, vct, 0.0).astype(BF16)


def _pos_features(pos_hi, pos_lo, n):
    feat = jnp.stack([pos_hi] * 3 + [pos_lo] * 3, axis=1).astype(F32)
    return jnp.pad(feat, ((0, 0), (AUG_POS0, AUG - AUG_POS0 - 6)))


def _compress(kvc, pos_k, pos_v, k_w1, k_w2, v_w1, v_w2):
    n_chunk = SEQ // CMP_STRIDE
    x = (kvc.reshape(BATCH, n_chunk, CMP_STRIDE, 2 * N_KV_GROUPS, HEAD_DIM)
         .transpose(0, 3, 1, 2, 4).reshape(BATCH, 2 * N_KV_GROUPS, n_chunk, CHUNK_FEAT))
    pos = jnp.stack([pos_k, pos_v])
    posa = pos[:, :CMP_STRIDE].reshape(2, 1, CHUNK_FEAT)
    posb = pos[:, CMP_STRIDE:].reshape(2, 1, CHUNK_FEAT)
    w1 = jnp.stack([k_w1, v_w1]).astype(BF16)
    w1a = w1[:, :CMP_STRIDE].reshape(2, CHUNK_FEAT, CMP_HIDDEN)
    w1b = w1[:, CMP_STRIDE:].reshape(2, CHUNK_FEAT, CMP_HIDDEN)
    w2k = jnp.pad(k_w2, ((0, 0), (0, AUG - HEAD_DIM))).astype(BF16)
    w2vt = v_w2.T.astype(BF16)
    nb = jnp.arange(N_CMP_PAD)
    posc = _pos_features(nb // 4, (nb % 4) * CMP_STRIDE + 0.5 * (CMP_BLOCK - 1), N_CMP_PAD)
    full = lambda shape: pl.BlockSpec(shape, lambda b, g: (0,) * len(shape))
    return pl.pallas_call(
        _cmp_kernel,
        grid=(BATCH, N_KV_GROUPS),
        in_specs=[
            pl.BlockSpec((1, 1, n_chunk, CHUNK_FEAT), lambda b, g: (b, g, 0, 0)),
            pl.BlockSpec((1, 1, n_chunk, CHUNK_FEAT), lambda b, g: (b, N_KV_GROUPS + g, 0, 0)),
            full((2, 1, CHUNK_FEAT)),
            full((2, 1, CHUNK_FEAT)),
            full((2, CHUNK_FEAT, CMP_HIDDEN)),
            full((2, CHUNK_FEAT, CMP_HIDDEN)),
            full((CMP_HIDDEN, AUG)),
            full((HEAD_DIM, CMP_HIDDEN)),
            full((N_CMP_PAD, AUG)),
        ],
        out_specs=[
            pl.BlockSpec((1, 1, N_CMP_PAD, AUG), lambda b, g: (b, g, 0, 0)),
            pl.BlockSpec((1, 1, HEAD_DIM, N_CMP_PAD), lambda b, g: (b, g, 0, 0)),
        ],
        out_shape=[
            jax.ShapeDtypeStruct((BATCH, N_KV_GROUPS, N_CMP_PAD, AUG), BF16),
            jax.ShapeDtypeStruct((BATCH, N_KV_GROUPS, HEAD_DIM, N_CMP_PAD), BF16),
        ],
        compiler_params=_cparams(2),
        name="nsa_compress",
    )(x, x, posa, posb, w1a, w1b, w2k, w2vt, posc)


def _select_top_n(score):
    n_sub = N_SEL_BLOCKS // SUBLANES
    pieces = [score[SUBLANES * v:SUBLANES * (v + 1), :] for v in range(n_sub)]
    ranks = [jnp.zeros((SUBLANES, TQ), F32) for _ in range(n_sub)]
    sub = lax.broadcasted_iota(jnp.int32, (SUBLANES, TQ), 0)
    for j2 in range(N_SEL_BLOCKS):
        v2, u2 = divmod(j2, SUBLANES)
        other = jnp.broadcast_to(score[j2:j2 + 1, :], (SUBLANES, TQ))
        for v in range(n_sub):
            if v < v2:
                beats = other > pieces[v]
            elif v > v2:
                beats = other >= pieces[v]
            else:
                beats = (other > pieces[v]) | ((other == pieces[v]) & (sub > u2))
            ranks[v] = ranks[v] + jnp.where(beats, 1.0, 0.0)
    return jnp.concatenate(ranks, axis=0) < N_SELECT


def _attn_kernel(qt_ref, kk_ref, vt_ref, kc_ref, vct_ref, gt_ref, rs_ref, rw_ref, consts_ref, constw_ref,
                 ovlt_ref, fixed_ref, cmpd_ref, diagd_ref, wmask_ref,
                 o_ref, ksa_ref, kwa_ref, vst_ref, vwt_ref, wq_ref, m_ref, acc_ref, tile_flag_ref):
    qi = pl.program_id(2)
    q0 = qi * TQ2
    n_vt = SEQ // LANES
    n_wpad = WINDOW // LANES

    @pl.when(qi == 0)
    def _():
        rows = 512

        def fill(i, carry):
            r0 = pl.multiple_of(i * rows, rows)
            kk = kk_ref[0, pl.ds(r0, rows), :]
            ksa_ref[pl.ds(r0, rows), :] = (jnp.dot(kk, rs_ref[...], preferred_element_type=F32)
                                           + consts_ref[pl.ds(r0, rows), :].astype(F32)).astype(BF16)
            kwa_ref[pl.ds(WINDOW + r0, rows), :] = (jnp.dot(kk, rw_ref[...], preferred_element_type=F32)
                                                    + constw_ref[pl.ds(r0, rows), :].astype(F32)).astype(BF16)
            return carry

        lax.fori_loop(0, SEQ // rows, fill, 0)
        kwa_ref[0:WINDOW, :] = (lax.broadcasted_iota(jnp.int32, (WINDOW, AUG), 1) == AUG_PAD).astype(BF16)
        ones_pad = (lax.broadcasted_iota(jnp.int32, (n_vt, V_ROWS - HEAD_DIM, LANES), 1) == 0).astype(BF16)
        vst_ref[:, 0:HEAD_DIM, :] = vt_ref[0, :, 0:HEAD_DIM, :]
        vst_ref[:, HEAD_DIM:, :] = ones_pad
        vwt_ref[0:n_wpad] = jnp.zeros((n_wpad, V_ROWS, LANES), BF16)
        vwt_ref[n_wpad:, 0:HEAD_DIM, :] = vt_ref[0, :, HEAD_DIM:, :]
        vwt_ref[n_wpad:, HEAD_DIM:, :] = ones_pad
        wq_ref[AUG_SEL0:AUG_SEL0 + N_SEL_BLOCKS, :] = jnp.zeros((N_SEL_BLOCKS, N_CHAIN * HQ), BF16)
        wq_ref[AUG_POS0:AUG_POS0 + AUG_FIXED_ROWS, :] = fixed_ref[0]
        wq_ref[AUG_POS0 + AUG_FIXED_ROWS:, :] = jnp.zeros((AUG - AUG_POS0 - AUG_FIXED_ROWS, N_CHAIN * HQ), BF16)

    qt = qt_ref[0]
    for c in range(N_CHAIN):
        for r in range(HEADS_PER_GROUP):
            lane0 = c * HQ + r * TQ
            wq_ref[0:HEAD_DIM, lane0:lane0 + TQ] = qt[r * HEAD_DIM:(r + 1) * HEAD_DIM, c * TQ:(c + 1) * TQ]

    def wq_chain(c):
        return wq_ref[:, c * HQ:(c + 1) * HQ]

    lane_q = lax.broadcasted_iota(jnp.int32, (1, HQ), 1) & (TQ - 1)

    def run_jobs(jobs, ahead=2):
        scores = [None] * len(jobs)
        for i in range(min(ahead, len(jobs))):
            scores[i] = jobs[i][0]()
        for i, (_, soft, value) in enumerate(jobs):
            mid = soft(scores[i])
            if i + ahead < len(jobs):
                scores[i + ahead] = jobs[i + ahead][0]()
            value(mid)

    def v_tiles(v_ref, row0, tk):
        vt0 = row0 >> _log2(LANES)
        return jnp.concatenate([v_ref[vt0 + j] for j in range(tk // LANES)], axis=1)

    oc_t = [None] * N_CHAIN
    imp_t = [None] * N_CHAIN

    def cmp_job(c):
        q0c = q0 + c * TQ

        def score():
            return jnp.dot(kc_ref[0, 0], wq_chain(c), preferred_element_type=F32)

        def soft(s):
            s = jnp.where(cmpd_ref[...] <= q0c, s, NEG_INF)
            p = jnp.exp2(s - jnp.max(s, axis=0, keepdims=True))
            has_c = (q0c + lane_q >= CMP_BLOCK - 1).astype(F32)
            p = p * (has_c / jnp.sum(p, axis=0, keepdims=True))
            psum = p[:, 0:TQ] + p[:, TQ:2 * TQ] + p[:, 2 * TQ:3 * TQ] + p[:, 3 * TQ:4 * TQ]
            return p.astype(BF16), _split3(psum)

        def value(mid):
            pb, (p1, p2, p3) = mid
            oc_t[c] = jnp.dot(vct_ref[0, 0], pb, preferred_element_type=F32)
            ovlt = ovlt_ref[...]
            imp_t[c] = (jnp.dot(ovlt, p1, preferred_element_type=F32)
                        + jnp.dot(ovlt, p2, preferred_element_type=F32)
                        + jnp.dot(ovlt, p3, preferred_element_type=F32))

        return score, soft, value

    ow_t = [None] * N_CHAIN

    def win_job(c):
        row0 = pl.multiple_of(q0 + c * TQ, LANES)

        def score():
            return jnp.dot(kwa_ref[pl.ds(row0, TKW), :], wq_chain(c), preferred_element_type=F32)

        def soft(sc):
            sc = sc + wmask_ref[...]
            return jnp.exp2(sc - jnp.max(sc, axis=0, keepdims=True)).astype(BF16)

        def value(pt):
            acc = jnp.dot(v_tiles(vwt_ref, row0, TKW), pt, preferred_element_type=F32)
            ow_t[c] = acc[0:HEAD_DIM] / acc[HEAD_DIM:HEAD_DIM + 1]

        return score, soft, value

    run_jobs([cmp_job(c) for c in range(N_CHAIN)] + [win_job(c) for c in range(N_CHAIN)])

    sel_bias = []
    unselected = None
    for c in range(N_CHAIN):
        q0c = q0 + c * TQ
        jb = lax.broadcasted_iota(jnp.int32, (N_SEL_BLOCKS, TQ), 0)
        bt = (q0c + lax.broadcasted_iota(jnp.int32, (N_SEL_BLOCKS, TQ), 1)) >> _log2(SEL_BLOCK)
        forced = (jb == 0) | (jb == bt) | (jb == bt - 1)
        score = jnp.where(forced, FORCE_SCORE, jnp.where(jb <= bt, imp_t[c], -FORCE_SCORE))
        bias = jnp.where(_select_top_n(score), 0.0, NEG_INF)
        unselected = bias if unselected is None else jnp.maximum(unselected, bias)
        sel_bias.append(bias.astype(BF16))
    for t in range(SEQ // TKS):
        tile_bias = unselected[t * (TKS // SEL_BLOCK):(t + 1) * (TKS // SEL_BLOCK), :]
        tile_flag_ref[t] = jnp.max(tile_bias)
    for c in range(N_CHAIN):
        for r in range(HEADS_PER_GROUP):
            lane0 = c * HQ + r * TQ
            wq_ref[AUG_SEL0:AUG_SEL0 + N_SEL_BLOCKS, lane0:lane0 + TQ] = sel_bias[c]

    m_ref[...] = jnp.full_like(m_ref, NEG_INF)
    acc_ref[...] = jnp.zeros_like(acc_ref)

    def sel_job(c, row0, tk, diagonal):
        row0 = pl.multiple_of(row0, LANES)

        def score():
            return jnp.dot(ksa_ref[pl.ds(row0, tk), :], wq_chain(c), preferred_element_type=F32)

        def soft(sc):
            if diagonal:
                sc = jnp.where(diagd_ref[0:tk, :] <= TQ * c, sc, NEG_INF)
            m_old = m_ref[c]
            m_new = jnp.maximum(m_old, jnp.max(sc, axis=0, keepdims=True))
            m_ref[c] = m_new
            return jnp.exp2(sc - m_new).astype(BF16), jnp.exp2(m_old - m_new)

        def value(mid):
            pt, alpha = mid
            acc_ref[c] = alpha * acc_ref[c] + jnp.dot(v_tiles(vst_ref, row0, tk), pt, preferred_element_type=F32)

        return score, soft, value

    n_full = q0 >> _log2(TKS)

    def tile_jobs(tile):
        return [sel_job(c, tile * TKS + s * TKJ, TKJ, False) for s in range(TKS // TKJ) for c in range(N_CHAIN)]

    def tile_selected(tile):
        return tile_flag_ref[tile] == 0.0

    def sel_pair(kt2, carry):
        first, second = tile_selected(2 * kt2), tile_selected(2 * kt2 + 1)

        @pl.when(first & second)
        def _():
            run_jobs(tile_jobs(2 * kt2) + tile_jobs(2 * kt2 + 1), ahead=4)

        @pl.when(first & jnp.logical_not(second))
        def _():
            run_jobs(tile_jobs(2 * kt2), ahead=4)

        @pl.when(jnp.logical_not(first) & second)
        def _():
            run_jobs(tile_jobs(2 * kt2 + 1), ahead=4)

        return carry

    lax.fori_loop(0, n_full >> 1, sel_pair, 0)

    def end_jobs(with_odd):
        jobs = tile_jobs(n_full - 1) if with_odd else []
        return jobs + [sel_job(c, q0, TQ * (c + 1), True) for c in range(N_CHAIN)]

    with_odd_tile = ((n_full & 1) != 0) & tile_selected(jnp.maximum(n_full - 1, 0))
    for with_odd in (False, True):
        @pl.when(with_odd_tile == with_odd)
        def _():
            run_jobs(end_jobs(with_odd), ahead=3)

    os_t = []
    for c in range(N_CHAIN):
        acc = acc_ref[c]
        os_t.append(acc[0:HEAD_DIM] / acc[HEAD_DIM:HEAD_DIM + 1])

    gt = gt_ref[0]
    cols = []
    for c in range(N_CHAIN):
        blocks = []
        for r in range(HEADS_PER_GROUP):
            sl = slice(r * TQ, (r + 1) * TQ)
            g3 = gt[3 * r:3 * r + 3, c * TQ:(c + 1) * TQ]
            blocks.append(g3[0:1] * oc_t[c][:, sl] + g3[1:2] * os_t[c][:, sl] + g3[2:3] * ow_t[c][:, sl])
        cols.append(jnp.concatenate(blocks, axis=0))
    o_ref[0] = jnp.concatenate(cols, axis=1).T.astype(BF16)


def _attention(qt, kk, vt, kc, vct, gt):
    hw = HEADS_PER_GROUP * HEAD_DIM
    src = jnp.arange(LANES)
    col = jnp.arange(AUG)
    rs = (src[:, None] == col[None, :]).astype(BF16) * (src[:, None] < HEAD_DIM)
    rw = (src[:, None] - HEAD_DIM == col[None, :]).astype(BF16) * (src[:, None] >= HEAD_DIM)
    key = jnp.arange(SEQ)
    pos = _pos_features(key // SEL_BLOCK, key % SEL_BLOCK, SEQ)
    onehot = (col[None, :] - AUG_SEL0 == (key // SEL_BLOCK)[:, None]).astype(F32)
    consts = (pos + onehot).astype(BF16)
    constw = pos.astype(BF16)
    cmp_start = jnp.arange(N_CMP_PAD) * CMP_STRIDE
    sel_start = jnp.arange(N_SEL_BLOCKS) * SEL_BLOCK
    ovlt = ((cmp_start[None, :] <= sel_start[:, None] + SEL_BLOCK - 1)
            & (cmp_start[None, :] + CMP_BLOCK - 1 >= sel_start[:, None])
            & (jnp.arange(N_CMP_PAD)[None, :] < N_CMP)).astype(BF16)
    hh = jnp.arange(1, N_HEADS + 1, dtype=F32)
    slope = jnp.exp2(-8.0 * hh / N_HEADS) * LOG2E
    s1, s2, s3 = _split3(slope)
    parts = jnp.stack([s1, s2, s3]).astype(F32)
    rows = jnp.concatenate([parts * SEL_BLOCK, parts,
                            jnp.zeros((AUG_POS_ROWS - 6, N_HEADS), F32)])
    slt = jnp.repeat(rows.reshape(AUG_POS_ROWS, N_KV_GROUPS, HEADS_PER_GROUP).transpose(1, 0, 2),
                     TQ, axis=2)
    pad_rows = jnp.zeros((N_KV_GROUPS, AUG_POS_ROWS, HQ), F32).at[:, 0, :].set(NEG_INF)
    fixed = jnp.tile(jnp.concatenate([slt, pad_rows], axis=1), (1, 1, N_CHAIN)).astype(BF16)
    lane_i = jnp.arange(HQ) % TQ
    nb = jnp.arange(N_CMP_PAD)
    cmpd = jnp.where(nb[:, None] < N_CMP, nb[:, None] * CMP_STRIDE + (CMP_BLOCK - 1) - lane_i[None, :],
                     2 * SEQ).astype(jnp.int32)

    def key_mask(n_keys, lo, hi):
        d = jnp.arange(n_keys)[:, None] - lane_i[None, :]
        return jnp.where((d >= lo) & (d <= hi), 0.0, NEG_INF).astype(F32)

    diagd = (jnp.arange(TQ2)[:, None] - lane_i[None, :]).astype(jnp.int32)
    wmask = key_mask(TKW, 1, WINDOW)

    nq = SEQ // TQ2
    n_vt = SEQ // LANES
    const = lambda shape: pl.BlockSpec(shape, lambda b, g, i: (0,) * len(shape))
    return pl.pallas_call(
        _attn_kernel,
        grid=(BATCH, N_KV_GROUPS, nq),
        in_specs=[
            pl.BlockSpec((1, hw, TQ2), lambda b, g, i: (b, g, i)),
            pl.BlockSpec((1, SEQ, LANES), lambda b, g, i: (b, 0, g)),
            pl.BlockSpec((1, n_vt, LANES, LANES), lambda b, g, i: (b, 0, g, 0)),
            pl.BlockSpec((1, 1, N_CMP_PAD, AUG), lambda b, g, i: (b, g, 0, 0)),
            pl.BlockSpec((1, 1, HEAD_DIM, N_CMP_PAD), lambda b, g, i: (b, g, 0, 0)),
            pl.BlockSpec((1, GATE_ROWS, TQ2), lambda b, g, i: (b, g, i)),
            const((LANES, AUG)),
            const((LANES, AUG)),
            const((SEQ, AUG)),
            const((SEQ, AUG)),
            const((N_SEL_BLOCKS, N_CMP_PAD)),
            pl.BlockSpec((1, AUG_FIXED_ROWS, N_CHAIN * HQ), lambda b, g, i: (g, 0, 0)),
            const((N_CMP_PAD, HQ)),
            const((TQ2, HQ)),
            const((TKW, HQ)),
        ],
        out_specs=pl.BlockSpec((1, TQ2, hw), lambda b, g, i: (b, i, g)),
        out_shape=jax.ShapeDtypeStruct((BATCH, SEQ, D_MODEL), BF16),
        scratch_shapes=[
            pltpu.VMEM((SEQ, AUG), BF16),
            pltpu.VMEM((WINDOW + SEQ, AUG), BF16),
            pltpu.VMEM((n_vt, V_ROWS, LANES), BF16),
            pltpu.VMEM((WINDOW // LANES + n_vt, V_ROWS, LANES), BF16),
            pltpu.VMEM((AUG, N_CHAIN * HQ), BF16),
            pltpu.VMEM((N_CHAIN, 1, HQ), F32),
            pltpu.VMEM((N_CHAIN, V_ROWS, HQ), F32),
            pltpu.SMEM((SEQ // TKS,), F32),
        ],
        compiler_params=_cparams(3),
        name="nsa_attention",
    )(qt, kk, vt, kc, vct, gt, rs, rw, consts, constw, ovlt, fixed, cmpd, diagd, wmask)


def _out_kernel(o_ref, x_ref, mod_ref, w_ref, lng_ref, lnb_ref, y_ref):
    y = jnp.dot(o_ref[0], w_ref[...], preferred_element_type=F32)
    gate = mod_ref[0][2:3]
    y_ref[0] = _deepnorm_ln(x_ref[0], gate, y, lng_ref[...], lnb_ref[...])


def _out_proj(o, x, mod, w_out, ln_g, ln_b):
    row = lambda b, t: (b, t, 0)
    c2 = lambda b, t: (0, 0)
    return pl.pallas_call(
        _out_kernel,
        grid=(BATCH, SEQ // TM),
        in_specs=[
            pl.BlockSpec((1, TM, D_MODEL), row),
            pl.BlockSpec((1, TM, D_MODEL), row),
            pl.BlockSpec((1, 6, D_MODEL), lambda b, t: (b, 0, 0)),
            pl.BlockSpec((D_MODEL, D_MODEL), c2),
            pl.BlockSpec((1, D_MODEL), c2),
            pl.BlockSpec((1, D_MODEL), c2),
        ],
        out_specs=pl.BlockSpec((1, TM, D_MODEL), row),
        out_shape=jax.ShapeDtypeStruct((BATCH, SEQ, D_MODEL), F32),
        compiler_params=_cparams(2),
        name="nsa_out_proj",
    )(o, x, mod, w_out.astype(BF16), ln_g.reshape(1, D_MODEL), ln_b.reshape(1, D_MODEL))


def _nsa_layer(x, mod, w_in, pos_k, pos_v, k_w1, k_w2, v_w1, v_w2, w_out, ln_g, ln_b):
    qt, kk, vt, gt, kvc = _nsa_proj(x, mod, w_in)
    kc, vct = _compress(kvc, pos_k, pos_v, k_w1, k_w2, v_w1, v_w2)
    o = _attention(qt, kk, vt, kc, vct, gt)
    return _out_proj(o, x, mod, w_out, ln_g, ln_b)


def kernel(x, c, ada_w, ada_b, ln1_g, ln1_b, ln2_g, ln2_b, ffn_w_in, ffn_conv, ffn_w_out, conv_w_in, conv_w,
           conv_w_out, pool_w_in, pool_w_grp, pool_scale, pool_w_out, nsa_w_in, nsa_cmp_pos_k, nsa_cmp_pos_v,
           nsa_cmp_k_w1, nsa_cmp_k_w2, nsa_cmp_v_w1, nsa_cmp_v_w2, nsa_w_out):
    assert x.shape == (BATCH, SEQ, D_MODEL) and x.dtype == F32
    mods = _modulation(c, ada_w, ada_b).reshape(DEPTH, BATCH, 6, D_MODEL)
    for i in range(DEPTH):
        mod = mods[i]
        m, j = i % N_MIXERS, i // N_MIXERS
        if m == 0:
            x = _gated_layer("conv", x, mod, 0, conv_w_in[j], conv_w[j], conv_w_out[j], ln1_g[i], ln1_b[i])
        elif m == 1:
            x = _pool_layer(x, mod, pool_w_in[j], pool_w_grp[j], pool_scale[j], pool_w_out[j], ln1_g[i], ln1_b[i])
        else:
            x = _nsa_layer(x, mod, nsa_w_in[j], nsa_cmp_pos_k[j], nsa_cmp_pos_v[j], nsa_cmp_k_w1[j],
                           nsa_cmp_k_w2[j], nsa_cmp_v_w1[j], nsa_cmp_v_w2[j], nsa_w_out[j], ln1_g[i], ln1_b[i])
        x = _gated_layer("ffn", x, mod, 3, ffn_w_in[i], ffn_conv[i], ffn_w_out[i], ln2_g[i], ln2_b[i])
    return x
```

```python
import functools
import math

import jax
import jax.numpy as jnp
from jax import lax
from jax.experimental import pallas as pl
from jax.experimental.pallas import tpu as pltpu

F32 = jnp.float32
BF16 = jnp.bfloat16

D_MODEL = 1024
BATCH = 16
SEQ = 4096
DEPTH = 4
N_MIXERS = 3
CONV_WIDTH = 3
D_FF = 2816
POOL_WINDOWS = (2, 4, 8, 16)
N_POOL_GROUPS = 4
POOL_GROUP = D_MODEL // N_POOL_GROUPS
N_HEADS = 16
HEAD_DIM = 64
N_KV_GROUPS = 4
HEADS_PER_GROUP = 4
KV_WIDTH = N_KV_GROUPS * HEAD_DIM
CMP_BLOCK = 32
CMP_STRIDE = 16
CMP_HIDDEN = 256
SEL_BLOCK = 64
N_SELECT = 16
WINDOW = 512
N_BRANCH = 3
ALPHA = (2.0 * DEPTH) ** 0.25
LN_EPS = 1e-5
NEG_INF = -1e30
FORCE_SCORE = 1e4
LOG2E = math.log2(math.e)

N_CMP = (SEQ - CMP_BLOCK) // CMP_STRIDE + 1
N_CMP_PAD = 256
N_SEL_BLOCKS = SEQ // SEL_BLOCK
LANES = 128
SUBLANES = 8

TM = 512
FC = 256
CONV_HALO = 8
POOL_HALO = 16
TQ = 128
N_CHAIN = 4
TQ2 = N_CHAIN * TQ
HQ = HEADS_PER_GROUP * TQ
TKS = 512
TKJ = 256
TKW = WINDOW + TQ
VMEM_LIMIT = 56 * 1024 * 1024

AUG = 256
AUG_SEL0 = HEAD_DIM
AUG_POS0 = 2 * HEAD_DIM
AUG_POS_ROWS = 16
AUG_PAD = AUG_POS0 + AUG_POS_ROWS
AUG_FIXED_ROWS = 2 * AUG_POS_ROWS
V_ROWS = 80
GATE_ROWS = 16


def _cparams(n_axes):
    return pltpu.CompilerParams(dimension_semantics=("arbitrary",) * n_axes,
                                vmem_limit_bytes=VMEM_LIMIT)


def _log2(n):
    assert n & (n - 1) == 0
    return n.bit_length() - 1


def _gelu(x):
    c = math.sqrt(2.0 / math.pi)
    return x * (0.5 * (1.0 + jnp.tanh(c * (x + 0.044715 * (x * x * x)))))


def _deepnorm_ln(x, gate, y, ln_g, ln_b):
    r = ALPHA * x + (1.0 + gate) * y
    mu = jnp.mean(r, axis=-1, keepdims=True)
    d = r - mu
    var = jnp.mean(d * d, axis=-1, keepdims=True)
    return d * lax.rsqrt(var + LN_EPS) * ln_g + ln_b


def _split3(a):
    a1 = a.astype(BF16)
    r1 = a - a1.astype(F32)
    a2 = r1.astype(BF16)
    a3 = (r1 - a2.astype(F32)).astype(BF16)
    return a1, a2, a3


def _ada_kernel(c_ref, w_ref, b_ref, o_ref):
    c = c_ref[...]
    cond = c * jax.nn.sigmoid(c)
    c1, c2, c3 = _split3(cond)
    w1, w2, w3 = _split3(w_ref[0])
    acc = jnp.dot(c1, w1, preferred_element_type=F32)
    acc += jnp.dot(c1, w2, preferred_element_type=F32)
    acc += jnp.dot(c2, w1, preferred_element_type=F32)
    acc += jnp.dot(c1, w3, preferred_element_type=F32)
    acc += jnp.dot(c2, w2, preferred_element_type=F32)
    acc += jnp.dot(c3, w1, preferred_element_type=F32)
    o_ref[0] = acc + b_ref[0]


def _modulation(c, ada_w, ada_b):
    nc = 1536
    n_col = 6 * D_MODEL // nc
    return pl.pallas_call(
        _ada_kernel,
        grid=(DEPTH, n_col),
        in_specs=[
            pl.BlockSpec((BATCH, D_MODEL), lambda i, j: (0, 0)),
            pl.BlockSpec((1, D_MODEL, nc), lambda i, j: (i, 0, j)),
            pl.BlockSpec((1, 1, nc), lambda i, j: (i, 0, j)),
        ],
        out_specs=pl.BlockSpec((1, BATCH, nc), lambda i, j: (i, 0, j)),
        out_shape=jax.ShapeDtypeStruct((DEPTH, BATCH, 6 * D_MODEL), F32),
        compiler_params=_cparams(2),
        name="modulation",
    )(c, ada_w, ada_b.reshape(DEPTH, 1, 6 * D_MODEL))


def _gated_kernel(mode, n_chunks, mrow, x_ref, mod_ref, win_ref, wconv_ref, wout_ref, lng_ref, lnb_ref,
                  o_ref, h_ref, carry_ref, cbuf_ref, acc_ref):
    t = pl.program_id(1)

    @pl.when(t == 0)
    def _():
        carry_ref[...] = jnp.zeros_like(carry_ref)

    m = mod_ref[0]
    sh, sc, gate = m[mrow:mrow + 1], m[mrow + 1:mrow + 2], m[mrow + 2:mrow + 3]
    feat = n_chunks * FC
    n_parts = 3 if mode == "conv" else 2

    xt = x_ref[0]
    h_ref[...] = (xt * (1.0 + sc) + sh).astype(BF16)

    def in_proj(j):
        h = h_ref[...]
        return [jnp.dot(h, win_ref[:, p * feat + j * FC:p * feat + (j + 1) * FC], preferred_element_type=F32)
                for p in range(n_parts)]

    p_next = in_proj(0)
    for j in range(n_chunks):
        p = p_next
        if mode == "conv":
            gate_in = p[0]
            cin = p[1] * p[2]
        else:
            cin, gate_in = p
        cbuf_ref[0:CONV_HALO, :] = carry_ref[j]
        cbuf_ref[CONV_HALO:, :] = cin
        carry_ref[j] = cin[TM - CONV_HALO:, :]
        w = wconv_ref[j]
        y = (w[0:1] * cbuf_ref[CONV_HALO - 2:CONV_HALO - 2 + TM, :]
             + w[1:2] * cbuf_ref[CONV_HALO - 1:CONV_HALO - 1 + TM, :]
             + w[2:3] * cin)
        if mode == "conv":
            z = gate_in * y
        else:
            z = _gelu(y) * gate_in
        z = z.astype(BF16)
        if j + 1 < n_chunks:
            p_next = in_proj(j + 1)
        out = jnp.dot(z, wout_ref[j], preferred_element_type=F32)
        if j == 0:
            acc_ref[...] = out
        else:
            acc_ref[...] += out

    o_ref[0] = _deepnorm_ln(xt, gate, acc_ref[...], lng_ref[...], lnb_ref[...])


def _gated_layer(mode, x, mod, mrow, w_in, w_conv, w_out, ln_g, ln_b):
    feat = w_out.shape[0]
    n_parts = w_in.shape[1] // feat
    n_chunks = feat // FC
    assert n_parts == (3 if mode == "conv" else 2)
    wout_r = w_out.astype(BF16).reshape(n_chunks, FC, D_MODEL)
    wconv_r = w_conv.reshape(CONV_WIDTH, n_chunks, FC).transpose(1, 0, 2)
    const3 = lambda b, t: (0, 0, 0)
    return pl.pallas_call(
        functools.partial(_gated_kernel, mode, n_chunks, mrow),
        grid=(BATCH, SEQ // TM),
        in_specs=[
            pl.BlockSpec((1, TM, D_MODEL), lambda b, t: (b, t, 0)),
            pl.BlockSpec((1, 6, D_MODEL), lambda b, t: (b, 0, 0)),
            pl.BlockSpec((D_MODEL, n_parts * feat), lambda b, t: (0, 0)),
            pl.BlockSpec((n_chunks, CONV_WIDTH, FC), const3),
            pl.BlockSpec((n_chunks, FC, D_MODEL), const3),
            pl.BlockSpec((1, D_MODEL), lambda b, t: (0, 0)),
            pl.BlockSpec((1, D_MODEL), lambda b, t: (0, 0)),
        ],
        out_specs=pl.BlockSpec((1, TM, D_MODEL), lambda b, t: (b, t, 0)),
        out_shape=jax.ShapeDtypeStruct((BATCH, SEQ, D_MODEL), F32),
        scratch_shapes=[
            pltpu.VMEM((TM, D_MODEL), BF16),
            pltpu.VMEM((n_chunks, CONV_HALO, FC), F32),
            pltpu.VMEM((CONV_HALO + TM, FC), F32),
            pltpu.VMEM((TM, D_MODEL), F32),
        ],
        compiler_params=_cparams(2),
        name="gated_" + mode,
    )(x, mod, w_in.astype(BF16), wconv_r, wout_r, ln_g.reshape(1, D_MODEL), ln_b.reshape(1, D_MODEL))


def _pool_kernel(x_ref, mod_ref, win_ref, wgrp_ref, scale_ref, wout_ref, lng_ref, lnb_ref,
                 o_ref, carry_ref, ubuf_ref, z_ref):
    t = pl.program_id(1)

    @pl.when(t == 0)
    def _():
        carry_ref[...] = jnp.zeros_like(carry_ref)

    xt = x_ref[0]
    m = mod_ref[0]
    sh, sc, gate = m[0:1], m[1:2], m[2:3]
    h = (xt * (1.0 + sc) + sh).astype(BF16)
    u = jnp.dot(h, win_ref[...], preferred_element_type=F32)
    ubuf_ref[0:POOL_HALO, :] = carry_ref[...]
    ubuf_ref[POOL_HALO:, :] = u
    carry_ref[...] = u[TM - POOL_HALO:, :]
    pos = t * TM + lax.broadcasted_iota(jnp.int32, (TM, 1), 0)
    for g, w in enumerate(POOL_WINDOWS):
        lo, hi = g * POOL_GROUP, (g + 1) * POOL_GROUP
        e = ubuf_ref[:, lo:hi]
        s = e
        k = 1
        while k < w:
            s = s + pltpu.roll(s, k, axis=0)
            k *= 2
        cnt = jnp.minimum(pos + 1, w).astype(F32)
        pooled = s[POOL_HALO:, :] / cnt - e[POOL_HALO:, :]
        zg = jnp.dot(pooled.astype(BF16), wgrp_ref[g], preferred_element_type=F32)
        z_ref[:, lo:hi] = (zg * scale_ref[:, lo:hi]).astype(BF16)
    y = jnp.dot(z_ref[...], wout_ref[...], preferred_element_type=F32)
    o_ref[0] = _deepnorm_ln(xt, gate, y, lng_ref[...], lnb_ref[...])


def _pool_layer(x, mod, w_in, w_grp, scale, w_out, ln_g, ln_b):
    c2 = lambda b, t: (0, 0)
    return pl.pallas_call(
        _pool_kernel,
        grid=(BATCH, SEQ // TM),
        in_specs=[
            pl.BlockSpec((1, TM, D_MODEL), lambda b, t: (b, t, 0)),
            pl.BlockSpec((1, 6, D_MODEL), lambda b, t: (b, 0, 0)),
            pl.BlockSpec((D_MODEL, D_MODEL), c2),
            pl.BlockSpec((N_POOL_GROUPS, POOL_GROUP, POOL_GROUP), lambda b, t: (0, 0, 0)),
            pl.BlockSpec((1, D_MODEL), c2),
            pl.BlockSpec((D_MODEL, D_MODEL), c2),
            pl.BlockSpec((1, D_MODEL), c2),
            pl.BlockSpec((1, D_MODEL), c2),
        ],
        out_specs=pl.BlockSpec((1, TM, D_MODEL), lambda b, t: (b, t, 0)),
        out_shape=jax.ShapeDtypeStruct((BATCH, SEQ, D_MODEL), F32),
        scratch_shapes=[
            pltpu.VMEM((POOL_HALO, D_MODEL), F32),
            pltpu.VMEM((POOL_HALO + TM, D_MODEL), F32),
            pltpu.VMEM((TM, D_MODEL), BF16),
        ],
        compiler_params=_cparams(2),
        name="pool_mixer",
    )(x, mod, w_in.astype(BF16), w_grp.astype(BF16), scale.reshape(1, D_MODEL), w_out.astype(BF16),
      ln_g.reshape(1, D_MODEL), ln_b.reshape(1, D_MODEL))


NSA_T_ROWS = D_MODEL + 2 * KV_WIDTH + N_KV_GROUPS * GATE_ROWS
Q_SCALE = HEAD_DIM ** -0.5 * LOG2E


def _nsa_proj_kernel(x_ref, mod_ref, wn_ref, wt_ref, qt_ref, kk_ref, vt_ref, gt_ref, kvc_ref):
    xt = x_ref[0]
    m = mod_ref[0]
    sh, sc = m[0:1], m[1:2]
    h = (xt * (1.0 + sc) + sh).astype(BF16)
    rn = jnp.dot(h, wn_ref[...], preferred_element_type=F32)
    kk_ref[0] = rn[:, :2 * KV_WIDTH].astype(BF16)
    kvc_ref[0] = rn[:, 2 * KV_WIDTH:]
    rt = lax.dot_general(wt_ref[...], h, (((1,), (1,)), ((), ())), preferred_element_type=F32)
    qt_ref[0] = (rt[0:D_MODEL] * Q_SCALE).astype(BF16)
    v_t = rt[D_MODEL:D_MODEL + 2 * KV_WIDTH].astype(BF16)
    for j in range(TM // LANES):
        vt_ref[0, j] = v_t[:, j * LANES:(j + 1) * LANES]
    gt_ref[0] = jax.nn.sigmoid(rt[D_MODEL + 2 * KV_WIDTH:])


def _nsa_proj(x, mod, w_in):
    kc0 = D_MODEL
    vc0, ks0, vs0, kw0, vw0, gl0 = (kc0 + i * KV_WIDTH for i in range(1, 7))
    grp = lambda c0, g: jnp.arange(c0 + g * HEAD_DIM, c0 + (g + 1) * HEAD_DIM)
    cols_n = jnp.concatenate([jnp.concatenate([grp(ks0, g), grp(kw0, g)]) for g in range(N_KV_GROUPS)]
                             + [jnp.arange(kc0, ks0)])
    wn = jnp.take(w_in, cols_n, axis=1).astype(BF16)
    cols_v = jnp.concatenate([jnp.concatenate([grp(vs0, g), grp(vw0, g)]) for g in range(N_KV_GROUPS)])
    n_gate = HEADS_PER_GROUP * N_BRANCH
    w_gate = w_in[:, gl0:gl0 + N_KV_GROUPS * n_gate].reshape(D_MODEL, N_KV_GROUPS, n_gate)
    w_gate = jnp.pad(w_gate, ((0, 0), (0, 0), (0, GATE_ROWS - n_gate))).reshape(D_MODEL, N_KV_GROUPS * GATE_ROWS)
    wt = jnp.concatenate([w_in[:, :D_MODEL], jnp.take(w_in, cols_v, axis=1), w_gate], axis=1).T.astype(BF16)
    row = lambda b, t: (b, t, 0)
    colt = lambda b, t: (b, 0, t)
    return pl.pallas_call(
        _nsa_proj_kernel,
        grid=(BATCH, SEQ // TM),
        in_specs=[
            pl.BlockSpec((1, TM, D_MODEL), row),
            pl.BlockSpec((1, 6, D_MODEL), lambda b, t: (b, 0, 0)),
            pl.BlockSpec((D_MODEL, 4 * KV_WIDTH), lambda b, t: (0, 0)),
            pl.BlockSpec((NSA_T_ROWS, D_MODEL), lambda b, t: (0, 0)),
        ],
        out_specs=[
            pl.BlockSpec((1, D_MODEL, TM), colt),
            pl.BlockSpec((1, TM, 2 * KV_WIDTH), row),
            pl.BlockSpec((1, TM // LANES, 2 * KV_WIDTH, LANES), lambda b, t: (b, t, 0, 0)),
            pl.BlockSpec((1, N_KV_GROUPS * GATE_ROWS, TM), colt),
            pl.BlockSpec((1, TM, 2 * KV_WIDTH), row),
        ],
        out_shape=[
            jax.ShapeDtypeStruct((BATCH, D_MODEL, SEQ), BF16),
            jax.ShapeDtypeStruct((BATCH, SEQ, 2 * KV_WIDTH), BF16),
            jax.ShapeDtypeStruct((BATCH, SEQ // LANES, 2 * KV_WIDTH, LANES), BF16),
            jax.ShapeDtypeStruct((BATCH, N_KV_GROUPS * GATE_ROWS, SEQ), F32),
            jax.ShapeDtypeStruct((BATCH, SEQ, 2 * KV_WIDTH), F32),
        ],
        compiler_params=_cparams(2),
        name="nsa_proj",
    )(x, mod, wn, wt)


CHUNK_FEAT = CMP_STRIDE * HEAD_DIM


def _cmp_hidden(x_ref, posa_ref, posb_ref, w1a_ref, w1b_ref):
    x = x_ref[0, 0]
    xa = (x + posa_ref[0]).astype(BF16)
    xb = (x + posb_ref[0]).astype(BF16)
    a = jnp.dot(xa, w1a_ref[0], preferred_element_type=F32)
    b = jnp.dot(xb, w1b_ref[0], preferred_element_type=F32)
    return _gelu(a + pltpu.roll(b, N_CMP_PAD - 1, axis=0)).astype(BF16)


def _cmp_kernel(xk_ref, xv_ref, posa_ref, posb_ref, w1a_ref, w1b_ref, w2k_ref, w2vt_ref, posc_ref, kc_ref, vct_ref):
    hk = _cmp_hidden(xk_ref, posa_ref.at[0:1], posb_ref.at[0:1], w1a_ref.at[0:1], w1b_ref.at[0:1])
    kc_ref[0, 0] = (jnp.dot(hk, w2k_ref[...], preferred_element_type=F32) + posc_ref[...]).astype(BF16)
    hv = _cmp_hidden(xv_ref, posa_ref.at[1:2], posb_ref.at[1:2], w1a_ref.at[1:2], w1b_ref.at[1:2])
    vct = lax.dot_general(w2vt_ref[...], hv, (((1,), (1,)), ((), ())), preferred_element_type=F32)
    n = lax.broadcasted_iota(jnp.int32, vct.shape, 1)
    vct_ref[0, 0] = jnp.where(n < N_CMP, vct, 0.0).astype(BF16)


def _pos_features(pos_hi, pos_lo, n):
    feat = jnp.stack([pos_hi] * 3 + [pos_lo] * 3, axis=1).astype(F32)
    return jnp.pad(feat, ((0, 0), (AUG_POS0, AUG - AUG_POS0 - 6)))


def _compress(kvc, pos_k, pos_v, k_w1, k_w2, v_w1, v_w2):
    n_chunk = SEQ // CMP_STRIDE
    x = (kvc.reshape(BATCH, n_chunk, CMP_STRIDE, 2 * N_KV_GROUPS, HEAD_DIM)
         .transpose(0, 3, 1, 2, 4).reshape(BATCH, 2 * N_KV_GROUPS, n_chunk, CHUNK_FEAT))
    pos = jnp.stack([pos_k, pos_v])
    posa = pos[:, :CMP_STRIDE].reshape(2, 1, CHUNK_FEAT)
    posb = pos[:, CMP_STRIDE:].reshape(2, 1, CHUNK_FEAT)
    w1 = jnp.stack([k_w1, v_w1]).astype(BF16)
    w1a = w1[:, :CMP_STRIDE].reshape(2, CHUNK_FEAT, CMP_HIDDEN)
    w1b = w1[:, CMP_STRIDE:].reshape(2, CHUNK_FEAT, CMP_HIDDEN)
    w2k = jnp.pad(k_w2, ((0, 0), (0, AUG - HEAD_DIM))).astype(BF16)
    w2vt = v_w2.T.astype(BF16)
    nb = jnp.arange(N_CMP_PAD)
    posc = _pos_features(nb // 4, (nb % 4) * CMP_STRIDE + 0.5 * (CMP_BLOCK - 1), N_CMP_PAD)
    full = lambda shape: pl.BlockSpec(shape, lambda b, g: (0,) * len(shape))
    return pl.pallas_call(
        _cmp_kernel,
        grid=(BATCH, N_KV_GROUPS),
        in_specs=[
            pl.BlockSpec((1, 1, n_chunk, CHUNK_FEAT), lambda b, g: (b, g, 0, 0)),
            pl.BlockSpec((1, 1, n_chunk, CHUNK_FEAT), lambda b, g: (b, N_KV_GROUPS + g, 0, 0)),
            full((2, 1, CHUNK_FEAT)),
            full((2, 1, CHUNK_FEAT)),
            full((2, CHUNK_FEAT, CMP_HIDDEN)),
            full((2, CHUNK_FEAT, CMP_HIDDEN)),
            full((CMP_HIDDEN, AUG)),
            full((HEAD_DIM, CMP_HIDDEN)),
            full((N_CMP_PAD, AUG)),
        ],
        out_specs=[
            pl.BlockSpec((1, 1, N_CMP_PAD, AUG), lambda b, g: (b, g, 0, 0)),
            pl.BlockSpec((1, 1, HEAD_DIM, N_CMP_PAD), lambda b, g: (b, g, 0, 0)),
        ],
        out_shape=[
            jax.ShapeDtypeStruct((BATCH, N_KV_GROUPS, N_CMP_PAD, AUG), BF16),
            jax.ShapeDtypeStruct((BATCH, N_KV_GROUPS, HEAD_DIM, N_CMP_PAD), BF16),
        ],
        compiler_params=_cparams(2),
        name="nsa_compress",
    )(x, x, posa, posb, w1a, w1b, w2k, w2vt, posc)


def _select_top_n(score):
    n_sub = N_SEL_BLOCKS // SUBLANES
    pieces = [score[SUBLANES * v:SUBLANES * (v + 1), :] for v in range(n_sub)]
    ranks = [jnp.zeros((SUBLANES, TQ), F32) for _ in range(n_sub)]
    sub = lax.broadcasted_iota(jnp.int32, (SUBLANES, TQ), 0)
    for j2 in range(N_SEL_BLOCKS):
        v2, u2 = divmod(j2, SUBLANES)
        other = jnp.broadcast_to(score[j2:j2 + 1, :], (SUBLANES, TQ))
        for v in range(n_sub):
            if v < v2:
                beats = other > pieces[v]
            elif v > v2:
                beats = other >= pieces[v]
            else:
                beats = (other > pieces[v]) | ((other == pieces[v]) & (sub > u2))
            ranks[v] = ranks[v] + jnp.where(beats, 1.0, 0.0)
    return jnp.concatenate(ranks, axis=0) < N_SELECT


def _attn_kernel(qt_ref, kk_ref, vt_ref, kc_ref, vct_ref, gt_ref, rs_ref, rw_ref, consts_ref, constw_ref,
                 ovlt_ref, fixed_ref, cmpd_ref, diagd_ref, wmask_ref,
                 o_ref, ksa_ref, kwa_ref, vst_ref, vwt_ref, wq_ref, m_ref, acc_ref, tile_flag_ref):
    qi = pl.program_id(2)
    q0 = qi * TQ2
    n_vt = SEQ // LANES
    n_wpad = WINDOW // LANES

    @pl.when(qi == 0)
    def _():
        rows = 512

        def fill(i, carry):
            r0 = pl.multiple_of(i * rows, rows)
            kk = kk_ref[0, pl.ds(r0, rows), :]
            ksa_ref[pl.ds(r0, rows), :] = (jnp.dot(kk, rs_ref[...], preferred_element_type=F32)
                                           + consts_ref[pl.ds(r0, rows), :].astype(F32)).astype(BF16)
            kwa_ref[pl.ds(WINDOW + r0, rows), :] = (jnp.dot(kk, rw_ref[...], preferred_element_type=F32)
                                                    + constw_ref[pl.ds(r0, rows), :].astype(F32)).astype(BF16)
            return carry

        lax.fori_loop(0, SEQ // rows, fill, 0)
        kwa_ref[0:WINDOW, :] = (lax.broadcasted_iota(jnp.int32, (WINDOW, AUG), 1) == AUG_PAD).astype(BF16)
        ones_pad = (lax.broadcasted_iota(jnp.int32, (n_vt, V_ROWS - HEAD_DIM, LANES), 1) == 0).astype(BF16)
        vst_ref[:, 0:HEAD_DIM, :] = vt_ref[0, :, 0:HEAD_DIM, :]
        vst_ref[:, HEAD_DIM:, :] = ones_pad
        vwt_ref[0:n_wpad] = jnp.zeros((n_wpad, V_ROWS, LANES), BF16)
        vwt_ref[n_wpad:, 0:HEAD_DIM, :] = vt_ref[0, :, HEAD_DIM:, :]
        vwt_ref[n_wpad:, HEAD_DIM:, :] = ones_pad
        wq_ref[AUG_SEL0:AUG_SEL0 + N_SEL_BLOCKS, :] = jnp.zeros((N_SEL_BLOCKS, N_CHAIN * HQ), BF16)
        wq_ref[AUG_POS0:AUG_POS0 + AUG_FIXED_ROWS, :] = fixed_ref[0]
        wq_ref[AUG_POS0 + AUG_FIXED_ROWS:, :] = jnp.zeros((AUG - AUG_POS0 - AUG_FIXED_ROWS, N_CHAIN * HQ), BF16)

    qt = qt_ref[0]
    for c in range(N_CHAIN):
        for r in range(HEADS_PER_GROUP):
            lane0 = c * HQ + r * TQ
            wq_ref[0:HEAD_DIM, lane0:lane0 + TQ] = qt[r * HEAD_DIM:(r + 1) * HEAD_DIM, c * TQ:(c + 1) * TQ]

    def wq_chain(c):
        return wq_ref[:, c * HQ:(c + 1) * HQ]

    lane_q = lax.broadcasted_iota(jnp.int32, (1, HQ), 1) & (TQ - 1)

    def run_jobs(jobs, ahead=2):
        scores = [None] * len(jobs)
        for i in range(min(ahead, len(jobs))):
            scores[i] = jobs[i][0]()
        for i, (_, soft, value) in enumerate(jobs):
            mid = soft(scores[i])
            if i + ahead < len(jobs):
                scores[i + ahead] = jobs[i + ahead][0]()
            value(mid)

    def v_tiles(v_ref, row0, tk):
        vt0 = row0 >> _log2(LANES)
        return jnp.concatenate([v_ref[vt0 + j] for j in range(tk // LANES)], axis=1)

    oc_t = [None] * N_CHAIN
    imp_t = [None] * N_CHAIN

    def cmp_job(c):
        q0c = q0 + c * TQ

        def score():
            return jnp.dot(kc_ref[0, 0], wq_chain(c), preferred_element_type=F32)

        def soft(s):
            s = jnp.where(cmpd_ref[...] <= q0c, s, NEG_INF)
            p = jnp.exp2(s - jnp.max(s, axis=0, keepdims=True))
            has_c = (q0c + lane_q >= CMP_BLOCK - 1).astype(F32)
            p = p * (has_c / jnp.sum(p, axis=0, keepdims=True))
            psum = p[:, 0:TQ] + p[:, TQ:2 * TQ] + p[:, 2 * TQ:3 * TQ] + p[:, 3 * TQ:4 * TQ]
            return p.astype(BF16), _split3(psum)

        def value(mid):
            pb, (p1, p2, p3) = mid
            oc_t[c] = jnp.dot(vct_ref[0, 0], pb, preferred_element_type=F32)
            ovlt = ovlt_ref[...]
            imp_t[c] = (jnp.dot(ovlt, p1, preferred_element_type=F32)
                        + jnp.dot(ovlt, p2, preferred_element_type=F32)
                        + jnp.dot(ovlt, p3, preferred_element_type=F32))

        return score, soft, value

    ow_t = [None] * N_CHAIN

    def win_job(c):
        row0 = pl.multiple_of(q0 + c * TQ, LANES)

        def score():
            return jnp.dot(kwa_ref[pl.ds(row0, TKW), :], wq_chain(c), preferred_element_type=F32)

        def soft(sc):
            sc = sc + wmask_ref[...]
            return jnp.exp2(sc - jnp.max(sc, axis=0, keepdims=True)).astype(BF16)

        def value(pt):
            acc = jnp.dot(v_tiles(vwt_ref, row0, TKW), pt, preferred_element_type=F32)
            ow_t[c] = acc[0:HEAD_DIM] / acc[HEAD_DIM:HEAD_DIM + 1]

        return score, soft, value

    run_jobs([job(c) for c in range(N_CHAIN) for job in (cmp_job, win_job)], ahead=3)

    sel_bias = []
    unselected = None
    for c in range(N_CHAIN):
        q0c = q0 + c * TQ
        jb = lax.broadcasted_iota(jnp.int32, (N_SEL_BLOCKS, TQ), 0)
        bt = (q0c + lax.broadcasted_iota(jnp.int32, (N_SEL_BLOCKS, TQ), 1)) >> _log2(SEL_BLOCK)
        forced = (jb == 0) | (jb == bt) | (jb == bt - 1)
        score = jnp.where(forced, FORCE_SCORE, jnp.where(jb <= bt, imp_t[c], -FORCE_SCORE))
        bias = jnp.where(_select_top_n(score), 0.0, NEG_INF)
        unselected = bias if unselected is None else jnp.maximum(unselected, bias)
        sel_bias.append(bias.astype(BF16))
    for t in range(SEQ // TKS):
        tile_bias = unselected[t * (TKS // SEL_BLOCK):(t + 1) * (TKS // SEL_BLOCK), :]
        tile_flag_ref[t] = jnp.max(tile_bias)
    for c in range(N_CHAIN):
        for r in range(HEADS_PER_GROUP):
            lane0 = c * HQ + r * TQ
            wq_ref[AUG_SEL0:AUG_SEL0 + N_SEL_BLOCKS, lane0:lane0 + TQ] = sel_bias[c]

    m_ref[...] = jnp.full_like(m_ref, NEG_INF)
    acc_ref[...] = jnp.zeros_like(acc_ref)

    def sel_job(c, row0, tk, diagonal):
        row0 = pl.multiple_of(row0, LANES)

        def score():
            return jnp.dot(ksa_ref[pl.ds(row0, tk), :], wq_chain(c), preferred_element_type=F32)

        def soft(sc):
            if diagonal:
                sc = jnp.where(diagd_ref[0:tk, :] <= TQ * c, sc, NEG_INF)
            m_old = m_ref[c]
            m_new = jnp.maximum(m_old, jnp.max(sc, axis=0, keepdims=True))
            m_ref[c] = m_new
            return jnp.exp2(sc - m_new).astype(BF16), jnp.exp2(m_old - m_new)

        def value(mid):
            pt, alpha = mid
            acc_ref[c] = alpha * acc_ref[c] + jnp.dot(v_tiles(vst_ref, row0, tk), pt, preferred_element_type=F32)

        return score, soft, value

    n_full = q0 >> _log2(TKS)

    def tile_jobs(tile):
        return [sel_job(c, tile * TKS + s * TKJ, TKJ, False) for s in range(TKS // TKJ) for c in range(N_CHAIN)]

    def tile_selected(tile):
        return tile_flag_ref[tile] == 0.0

    def sel_pair(kt2, carry):
        first, second = tile_selected(2 * kt2), tile_selected(2 * kt2 + 1)

        @pl.when(first & second)
        def _():
            run_jobs(tile_jobs(2 * kt2) + tile_jobs(2 * kt2 + 1), ahead=4)

        @pl.when(first & jnp.logical_not(second))
        def _():
            run_jobs(tile_jobs(2 * kt2), ahead=4)

        @pl.when(jnp.logical_not(first) & second)
        def _():
            run_jobs(tile_jobs(2 * kt2 + 1), ahead=4)

        return carry

    lax.fori_loop(0, n_full >> 1, sel_pair, 0)

    def end_jobs(with_odd):
        jobs = tile_jobs(n_full - 1) if with_odd else []
        return jobs + [sel_job(c, q0, TQ * (c + 1), True) for c in range(N_CHAIN)]

    with_odd_tile = ((n_full & 1) != 0) & tile_selected(jnp.maximum(n_full - 1, 0))
    for with_odd in (False, True):
        @pl.when(with_odd_tile == with_odd)
        def _():
            run_jobs(end_jobs(with_odd), ahead=3)

    os_t = []
    for c in range(N_CHAIN):
        acc = acc_ref[c]
        os_t.append(acc[0:HEAD_DIM] / acc[HEAD_DIM:HEAD_DIM + 1])

    gt = gt_ref[0]
    cols = []
    for c in range(N_CHAIN):
        blocks = []
        for r in range(HEADS_PER_GROUP):
            sl = slice(r * TQ, (r + 1) * TQ)
            g3 = gt[3 * r:3 * r + 3, c * TQ:(c + 1) * TQ]
            blocks.append(g3[0:1] * oc_t[c][:, sl] + g3[1:2] * os_t[c][:, sl] + g3[2:3] * ow_t[c][:, sl])
        cols.append(jnp.concatenate(blocks, axis=0))
    o_ref[0] = jnp.concatenate(cols, axis=1).T.astype(BF16)


def _attention(qt, kk, vt, kc, vct, gt):
    hw = HEADS_PER_GROUP * HEAD_DIM
    src = jnp.arange(LANES)
    col = jnp.arange(AUG)
    rs = (src[:, None] == col[None, :]).astype(BF16) * (src[:, None] < HEAD_DIM)
    rw = (src[:, None] - HEAD_DIM == col[None, :]).astype(BF16) * (src[:, None] >= HEAD_DIM)
    key = jnp.arange(SEQ)
    pos = _pos_features(key // SEL_BLOCK, key % SEL_BLOCK, SEQ)
    onehot = (col[None, :] - AUG_SEL0 == (key // SEL_BLOCK)[:, None]).astype(F32)
    consts = (pos + onehot).astype(BF16)
    constw = pos.astype(BF16)
    cmp_start = jnp.arange(N_CMP_PAD) * CMP_STRIDE
    sel_start = jnp.arange(N_SEL_BLOCKS) * SEL_BLOCK
    ovlt = ((cmp_start[None, :] <= sel_start[:, None] + SEL_BLOCK - 1)
            & (cmp_start[None, :] + CMP_BLOCK - 1 >= sel_start[:, None])
            & (jnp.arange(N_CMP_PAD)[None, :] < N_CMP)).astype(BF16)
    hh = jnp.arange(1, N_HEADS + 1, dtype=F32)
    slope = jnp.exp2(-8.0 * hh / N_HEADS) * LOG2E
    s1, s2, s3 = _split3(slope)
    parts = jnp.stack([s1, s2, s3]).astype(F32)
    rows = jnp.concatenate([parts * SEL_BLOCK, parts,
                            jnp.zeros((AUG_POS_ROWS - 6, N_HEADS), F32)])
    slt = jnp.repeat(rows.reshape(AUG_POS_ROWS, N_KV_GROUPS, HEADS_PER_GROUP).transpose(1, 0, 2),
                     TQ, axis=2)
    pad_rows = jnp.zeros((N_KV_GROUPS, AUG_POS_ROWS, HQ), F32).at[:, 0, :].set(NEG_INF)
    fixed = jnp.tile(jnp.concatenate([slt, pad_rows], axis=1), (1, 1, N_CHAIN)).astype(BF16)
    lane_i = jnp.arange(HQ) % TQ
    nb = jnp.arange(N_CMP_PAD)
    cmpd = jnp.where(nb[:, None] < N_CMP, nb[:, None] * CMP_STRIDE + (CMP_BLOCK - 1) - lane_i[None, :],
                     2 * SEQ).astype(jnp.int32)

    def key_mask(n_keys, lo, hi):
        d = jnp.arange(n_keys)[:, None] - lane_i[None, :]
        return jnp.where((d >= lo) & (d <= hi), 0.0, NEG_INF).astype(F32)

    diagd = (jnp.arange(TQ2)[:, None] - lane_i[None, :]).astype(jnp.int32)
    wmask = key_mask(TKW, 1, WINDOW)

    nq = SEQ // TQ2
    n_vt = SEQ // LANES
    const = lambda shape: pl.BlockSpec(shape, lambda b, g, i: (0,) * len(shape))
    return pl.pallas_call(
        _attn_kernel,
        grid=(BATCH, N_KV_GROUPS, nq),
        in_specs=[
            pl.BlockSpec((1, hw, TQ2), lambda b, g, i: (b, g, i)),
            pl.BlockSpec((1, SEQ, LANES), lambda b, g, i: (b, 0, g)),
            pl.BlockSpec((1, n_vt, LANES, LANES), lambda b, g, i: (b, 0, g, 0)),
            pl.BlockSpec((1, 1, N_CMP_PAD, AUG), lambda b, g, i: (b, g, 0, 0)),
            pl.BlockSpec((1, 1, HEAD_DIM, N_CMP_PAD), lambda b, g, i: (b, g, 0, 0)),
            pl.BlockSpec((1, GATE_ROWS, TQ2), lambda b, g, i: (b, g, i)),
            const((LANES, AUG)),
            const((LANES, AUG)),
            const((SEQ, AUG)),
            const((SEQ, AUG)),
            const((N_SEL_BLOCKS, N_CMP_PAD)),
            pl.BlockSpec((1, AUG_FIXED_ROWS, N_CHAIN * HQ), lambda b, g, i: (g, 0, 0)),
            const((N_CMP_PAD, HQ)),
            const((TQ2, HQ)),
            const((TKW, HQ)),
        ],
        out_specs=pl.BlockSpec((1, TQ2, hw), lambda b, g, i: (b, i, g)),
        out_shape=jax.ShapeDtypeStruct((BATCH, SEQ, D_MODEL), BF16),
        scratch_shapes=[
            pltpu.VMEM((SEQ, AUG), BF16),
            pltpu.VMEM((WINDOW + SEQ, AUG), BF16),
            pltpu.VMEM((n_vt, V_ROWS, LANES), BF16),
            pltpu.VMEM((WINDOW // LANES + n_vt, V_ROWS, LANES), BF16),
            pltpu.VMEM((AUG, N_CHAIN * HQ), BF16),
            pltpu.VMEM((N_CHAIN, 1, HQ), F32),
            pltpu.VMEM((N_CHAIN, V_ROWS, HQ), F32),
            pltpu.SMEM((SEQ // TKS,), F32),
        ],
        compiler_params=_cparams(3),
        name="nsa_attention",
    )(qt, kk, vt, kc, vct, gt, rs, rw, consts, constw, ovlt, fixed, cmpd, diagd, wmask)


def _out_kernel(o_ref, x_ref, mod_ref, w_ref, lng_ref, lnb_ref, y_ref):
    y = jnp.dot(o_ref[0], w_ref[...], preferred_element_type=F32)
    gate = mod_ref[0][2:3]
    y_ref[0] = _deepnorm_ln(x_ref[0], gate, y, lng_ref[...], lnb_ref[...])


def _out_proj(o, x, mod, w_out, ln_g, ln_b):
    row = lambda b, t: (b, t, 0)
    c2 = lambda b, t: (0, 0)
    return pl.pallas_call(
        _out_kernel,
        grid=(BATCH, SEQ // TM),
        in_specs=[
            pl.BlockSpec((1, TM, D_MODEL), row),
            pl.BlockSpec((1, TM, D_MODEL), row),
            pl.BlockSpec((1, 6, D_MODEL), lambda b, t: (b, 0, 0)),
            pl.BlockSpec((D_MODEL, D_MODEL), c2),
            pl.BlockSpec((1, D_MODEL), c2),
            pl.BlockSpec((1, D_MODEL), c2),
        ],
        out_specs=pl.BlockSpec((1, TM, D_MODEL), row),
        out_shape=jax.ShapeDtypeStruct((BATCH, SEQ, D_MODEL), F32),
        compiler_params=_cparams(2),
        name="nsa_out_proj",
    )(o, x, mod, w_out.astype(BF16), ln_g.reshape(1, D_MODEL), ln_b.reshape(1, D_MODEL))


def _nsa_layer(x, mod, w_in, pos_k, pos_v, k_w1, k_w2, v_w1, v_w2, w_out, ln_g, ln_b):
    qt, kk, vt, gt, kvc = _nsa_proj(x, mod, w_in)
    kc, vct = _compress(kvc, pos_k, pos_v, k_w1, k_w2, v_w1, v_w2)
    o = _attention(qt, kk, vt, kc, vct, gt)
    return _out_proj(o, x, mod, w_out, ln_g, ln_b)


def kernel(x, c, ada_w, ada_b, ln1_g, ln1_b, ln2_g, ln2_b, ffn_w_in, ffn_conv, ffn_w_out, conv_w_in, conv_w,
           conv_w_out, pool_w_in, pool_w_grp, pool_scale, pool_w_out, nsa_w_in, nsa_cmp_pos_k, nsa_cmp_pos_v,
           nsa_cmp_k_w1, nsa_cmp_k_w2, nsa_cmp_v_w1, nsa_cmp_v_w2, nsa_w_out):
    assert x.shape == (BATCH, SEQ, D_MODEL) and x.dtype == F32
    mods = _modulation(c, ada_w, ada_b).reshape(DEPTH, BATCH, 6, D_MODEL)
    for i in range(DEPTH):
        mod = mods[i]
        m, j = i % N_MIXERS, i // N_MIXERS
        if m == 0:
            x = _gated_layer("conv", x, mod, 0, conv_w_in[j], conv_w[j], conv_w_out[j], ln1_g[i], ln1_b[i])
        elif m == 1:
            x = _pool_layer(x, mod, pool_w_in[j], pool_w_grp[j], pool_scale[j], pool_w_out[j], ln1_g[i], ln1_b[i])
        else:
            x = _nsa_layer(x, mod, nsa_w_in[j], nsa_cmp_pos_k[j], nsa_cmp_pos_v[j], nsa_cmp_k_w1[j],
                           nsa_cmp_k_w2[j], nsa_cmp_v_w1[j], nsa_cmp_v_w2[j], nsa_w_out[j], ln1_g[i], ln1_b[i])
        x = _gated_layer("ffn", x, mod, 3, ffn_w_in[i], ffn_conv[i], ffn_w_out[i], ln2_g[i], ln2_b[i])
    return x
```
